```python
import jax
import jax.numpy as jnp
from jax import lax
import numpy as np

D_MODEL = 1024
BATCH = 4
SEQ = 4096
DEPTH = 4

N_MIXERS = 2
N_RWKV = (DEPTH + N_MIXERS - 1) // N_MIXERS
N_NSA = DEPTH // N_MIXERS
N_VRES = N_RWKV - 1

NORM_EPS = 1e-6
MLP_HIDDEN = 4 * D_MODEL

RWKV_HEAD_DIM = 64
RWKV_HEADS = D_MODEL // RWKV_HEAD_DIM
RWKV_GN_EPS = 64e-5
LORA_DECAY = max(32, int(round(1.8 * D_MODEL ** 0.5 / 32)) * 32)
LORA_ICLR = max(32, int(round(1.8 * D_MODEL ** 0.5 / 32)) * 32)
LORA_VRES = max(32, int(round(1.3 * D_MODEL ** 0.5 / 32)) * 32)
LORA_GATE = max(32, int(round(0.6 * D_MODEL ** 0.8 / 32)) * 32)

NSA_HEAD_DIM = 64
NSA_HEADS = D_MODEL // NSA_HEAD_DIM
NSA_KV_HEADS = NSA_HEADS // 4
NSA_GROUP = NSA_HEADS // NSA_KV_HEADS
CMP_BLOCK = 32
CMP_STRIDE = 16
CMP_HIDDEN = 2 * NSA_HEAD_DIM
SLC_BLOCK = 64
N_SELECT = 16
WINDOW = 512
Q_BLOCK = 64
SEL_BIG = 1e9
NSA_Q_COLS = NSA_HEADS * NSA_HEAD_DIM
NSA_KV_COLS = NSA_KV_HEADS * NSA_HEAD_DIM
NSA_IN_COLS = NSA_Q_COLS + 6 * NSA_KV_COLS + 3 * NSA_HEADS

kernel_name = "rwkv7_nsa_interleaved_hybrid"


def rms_norm(x, g):
    xf = x.astype(jnp.float32)
    y = xf * lax.rsqrt(jnp.mean(xf * xf, axis=-1, keepdims=True) + NORM_EPS)
    return (y * g.astype(jnp.float32)).astype(x.dtype)


def masked_softmax(s, mask):
    s = jnp.where(mask, s.astype(jnp.float32), -jnp.inf)
    m = jnp.max(s, axis=-1, keepdims=True)
    m = jnp.where(jnp.isfinite(m), m, 0.0)
    e = jnp.where(mask, jnp.exp(s - m), 0.0)
    return e / jnp.maximum(jnp.sum(e, axis=-1, keepdims=True), 1e-30)


def alibi_slopes(n):
    return jnp.exp2(-8.0 * (jnp.arange(n, dtype=jnp.float32) + 1.0) / n)


def squared_relu_mlp(x, w1, w2):
    return jnp.square(jax.nn.relu(x @ w1)) @ w2


def wkv7_scan(r, w, k, v, a, b):
    B, T, H, N = r.shape

    def step(S, inp):
        r_t, w_t, k_t, v_t, a_t, b_t = inp
        sa = jnp.einsum('bhij,bhj->bhi', S, a_t)
        S = S * w_t[:, :, None, :] + sa[..., None] * b_t[:, :, None, :] + v_t[..., None] * k_t[:, :, None, :]
        return S, jnp.einsum('bhij,bhj->bhi', S, r_t)

    seq = tuple(jnp.moveaxis(t.astype(jnp.float32), 1, 0) for t in (r, w, k, v, a, b))
    S0 = jnp.zeros((B, H, N, N), jnp.float32)
    _, y = lax.scan(step, S0, seq)
    return jnp.moveaxis(y, 0, 1)


def rwkv7_time_mix(x, mu, w_in, w0, w1, w2, a0, a1, a2, g1, g2, k_k, k_a, r_k, ln_w, ln_b, w_out,
                   v_first, vres):
    B, T, C = x.shape
    H, N = RWKV_HEADS, RWKV_HEAD_DIM
    dx = jnp.pad(x, ((0, 0), (1, 0), (0, 0)))[:, :-1] - x
    xr, xw, xk, xv, xa, xg = (x + dx * mu[i] for i in range(6))
    rkv = jnp.einsum('nbtc,cnf->nbtf', jnp.stack([xr, xk, xv]), w_in.reshape(C, 3, C))
    r, k, v = rkv[0], rkv[1], rkv[2]
    v_raw = v
    if vres is not None:
        v0, v1, v2 = vres
        v = v + (v_first - v) * jax.nn.sigmoid(v0 + (xv @ v1) @ v2)
    w_log = -jax.nn.softplus(-(w0 + jnp.tanh(xw @ w1) @ w2)) - 0.5
    decay = jnp.exp(-jnp.exp(w_log.astype(jnp.float32)))
    a = jax.nn.sigmoid(a0 + (xa @ a1) @ a2)
    g = jax.nn.sigmoid(xg @ g1) @ g2

    def heads(t):
        return t.reshape(B, T, H, N)

    kk = heads(k * k_k).astype(jnp.float32)
    kk = kk / jnp.maximum(jnp.sqrt(jnp.sum(kk * kk, axis=-1, keepdims=True)), 1e-12)
    k = k * (1 + (a - 1) * k_a)
    r_h, k_h, v_h = heads(r), heads(k), heads(v)
    y = wkv7_scan(r_h, heads(decay), k_h, v_h, -kk, kk * heads(a).astype(jnp.float32))
    mean = jnp.mean(y, axis=-1, keepdims=True)
    var = jnp.mean(jnp.square(y - mean), axis=-1, keepdims=True)
    y = ((y - mean) * lax.rsqrt(var + RWKV_GN_EPS)).reshape(B, T, C) * ln_w.astype(jnp.float32) \
        + ln_b.astype(jnp.float32)
    bonus = jnp.sum((r_h * k_h * r_k).astype(jnp.float32), axis=-1, keepdims=True) * v_h.astype(jnp.float32)
    y = (y + bonus.reshape(B, T, C)) * g.astype(jnp.float32)
    return y.astype(x.dtype) @ w_out, v_raw


def nsa_compress(t, pe, w1, w2):
    B, T, HKV, DH = t.shape
    tr = t.reshape(B, T // CMP_STRIDE, CMP_STRIDE, HKV, DH)
    blocks = jnp.concatenate([tr[:, :-1], tr[:, 1:]], axis=2) + pe[None, None, :, None, :]
    n_blk = blocks.shape[1]
    flat = jnp.moveaxis(blocks, 3, 2).reshape(B, n_blk, HKV, CMP_BLOCK * DH)
    return jax.nn.gelu(flat @ w1) @ w2


def nsa_mix(x, w_in, cmp_pe, cmp_w1, cmp_w2, q_norm, k_norm, w_out):
    B, T, C = x.shape
    H, HKV, G, DH = NSA_HEADS, NSA_KV_HEADS, NSA_GROUP, NSA_HEAD_DIM
    proj = x @ w_in
    q = rms_norm(proj[..., :NSA_Q_COLS].reshape(B, T, HKV, G, DH), q_norm)
    kvs = proj[..., NSA_Q_COLS:NSA_Q_COLS + 6 * NSA_KV_COLS].reshape(B, T, 6, HKV, DH)
    kc, vc, ks, vs, kw, vw = (kvs[:, :, i] for i in range(6))
    gates = jax.nn.sigmoid(proj[..., NSA_Q_COLS + 6 * NSA_KV_COLS:].astype(jnp.float32)).reshape(B, T, HKV, G, 3)

    k_cmp = rms_norm(nsa_compress(kc, cmp_pe[0], cmp_w1[0], cmp_w2[0]), k_norm[0])
    v_cmp = nsa_compress(vc, cmp_pe[1], cmp_w1[1], cmp_w2[1])
    n_blk = k_cmp.shape[1]
    cmp_start = jnp.arange(n_blk) * CMP_STRIDE
    cmp_end = cmp_start + CMP_BLOCK - 1

    n_slc = T // SLC_BLOCK
    n_sel = min(N_SELECT, n_slc)
    ks_blocks = rms_norm(ks, k_norm[1]).reshape(B, n_slc, SLC_BLOCK, HKV, DH).transpose(0, 3, 1, 2, 4)
    vs_blocks = vs.reshape(B, n_slc, SLC_BLOCK, HKV, DH).transpose(0, 3, 1, 2, 4)
    slc_start = jnp.arange(n_slc) * SLC_BLOCK
    overlap = ((cmp_start[:, None] <= slc_start[None, :] + SLC_BLOCK - 1)
               & (cmp_end[:, None] >= slc_start[None, :])).astype(jnp.float32)

    pad = ((0, 0), (WINDOW, 0), (0, 0), (0, 0))
    kw_p = jnp.pad(rms_norm(kw, k_norm[2]), pad)
    vw_p = jnp.pad(vw, pad)

    slopes = alibi_slopes(H).reshape(HKV, G)[None, :, :, None, None]
    scale = DH ** -0.5
    b_ix = jnp.arange(B)[:, None, None, None]
    h_ix = jnp.arange(HKV)[None, :, None, None]
    blk = jnp.arange(n_slc)

    def q_block(i):
        q0 = i * Q_BLOCK
        t = q0 + jnp.arange(Q_BLOCK)
        qb = lax.dynamic_slice_in_dim(q, q0, Q_BLOCK, axis=1)

        dist_c = t[:, None] - cmp_end[None, :]
        s = jnp.einsum('bqhgd,bnhd->bhgqn', qb, k_cmp) * scale - slopes * dist_c
        p_c = masked_softmax(s, dist_c >= 0)
        o_c = jnp.einsum('bhgqn,bnhd->bqhgd', p_c, v_cmp.astype(jnp.float32))

        imp = jnp.einsum('bhgqn,nj->bhqj', p_c, overlap)
        cur = (t // SLC_BLOCK)[:, None]
        forced = (blk[None, :] == 0) | (blk[None, :] == cur) | (blk[None, :] == cur - 1)
        imp = jnp.where(forced, SEL_BIG, imp)
        imp = jnp.where(slc_start[None, :] <= t[:, None], imp, -SEL_BIG)
        _, top_idx = lax.top_k(imp, n_sel)
        k_sel = ks_blocks[b_ix, h_ix, top_idx].reshape(B, HKV, Q_BLOCK, n_sel * SLC_BLOCK, DH)
        v_sel = vs_blocks[b_ix, h_ix, top_idx].reshape(B, HKV, Q_BLOCK, n_sel * SLC_BLOCK, DH)
        s_pos = (top_idx[..., None] * SLC_BLOCK + jnp.arange(SLC_BLOCK)).reshape(B, HKV, Q_BLOCK, n_sel * SLC_BLOCK)
        dist_s = (t[:, None] - s_pos)[:, :, None]
        s = jnp.einsum('bqhgd,bhqkd->bhgqk', qb, k_sel) * scale - slopes * dist_s
        p_s = masked_softmax(s, dist_s >= 0)
        o_s = jnp.einsum('bhgqk,bhqkd->bqhgd', p_s, v_sel.astype(jnp.float32))

        kwb = lax.dynamic_slice_in_dim(kw_p, q0, WINDOW + Q_BLOCK, axis=1)
        vwb = lax.dynamic_slice_in_dim(vw_p, q0, WINDOW + Q_BLOCK, axis=1)
        w_pos = q0 - WINDOW + jnp.arange(WINDOW + Q_BLOCK)
        dist_w = t[:, None] - w_pos[None, :]
        mask_w = (dist_w >= 0) & (dist_w < WINDOW) & (w_pos[None, :] >= 0)
        s = jnp.einsum('bqhgd,bkhd->bhgqk', qb, kwb) * scale - slopes * dist_w
        p_w = masked_softmax(s, mask_w)
        o_w = jnp.einsum('bhgqk,bkhd->bqhgd', p_w, vwb.astype(jnp.float32))

        gb = lax.dynamic_slice_in_dim(gates, q0, Q_BLOCK, axis=1)
        o = gb[..., 0:1] * o_c + gb[..., 1:2] * o_s + gb[..., 2:3] * o_w
        return o.astype(x.dtype)

    outs = lax.map(q_block, jnp.arange(T // Q_BLOCK))
    o = jnp.moveaxis(outs, 0, 1).reshape(B, T, C)
    return o @ w_out


def setup_inputs(seed: int = 0) -> dict:
    key = jax.random.key(seed)
    keys = iter(jax.random.split(key, 64))
    f32 = jnp.float32
    D = D_MODEL
    NA, NB, NV = N_RWKV, N_NSA, N_VRES

    def nrm(shape, scale):
        return scale * jax.random.normal(next(keys), shape, f32)

    def unif(shape, lo, hi):
        return jax.random.uniform(next(keys), shape, f32, lo, hi)

    return {
        "x": nrm((BATCH, SEQ, D), 1.0),
        "mix_norm": 1.0 + nrm((DEPTH, D), 0.05),
        "mlp_norm": 1.0 + nrm((DEPTH, D), 0.05),
        "mlp_w1": nrm((DEPTH, D, MLP_HIDDEN), D ** -0.5),
        "mlp_w2": nrm((DEPTH, MLP_HIDDEN, D), 0.5 * MLP_HIDDEN ** -0.5),
        "rwkv_mu": unif((NA, 6, D), 0.0, 1.0),
        "rwkv_w_in": nrm((NA, D, 3 * D), D ** -0.5),
        "rwkv_w0": unif((NA, D), -6.5, -1.5),
        "rwkv_w1": nrm((NA, D, LORA_DECAY), D ** -0.5),
        "rwkv_w2": nrm((NA, LORA_DECAY, D), 0.1 * LORA_DECAY ** -0.5),
        "rwkv_a0": nrm((NA, D), 0.1),
        "rwkv_a1": nrm((NA, D, LORA_ICLR), D ** -0.5),
        "rwkv_a2": nrm((NA, LORA_ICLR, D), 0.5 * LORA_ICLR ** -0.5),
        "rwkv_v0": 1.0 + nrm((NV, D), 0.1),
        "rwkv_v1": nrm((NV, D, LORA_VRES), D ** -0.5),
        "rwkv_v2": nrm((NV, LORA_VRES, D), 0.5 * LORA_VRES ** -0.5),
        "rwkv_g1": nrm((NA, D, LORA_GATE), D ** -0.5),
        "rwkv_g2": nrm((NA, LORA_GATE, D), LORA_GATE ** -0.5),
        "rwkv_k_k": 0.85 + nrm((NA, D), 0.02),
        "rwkv_k_a": 1.0 + nrm((NA, D), 0.02),
        "rwkv_r_k": nrm((NA, RWKV_HEADS, RWKV_HEAD_DIM), 0.1),
        "rwkv_ln_w": 1.0 + nrm((NA, D), 0.05),
        "rwkv_ln_b": nrm((NA, D), 0.02),
        "rwkv_w_out": nrm((NA, D, D), 0.5 * D ** -0.5),
        "nsa_w_in": nrm((NB, D, NSA_IN_COLS), D ** -0.5),
        "nsa_cmp_pe": nrm((NB, 2, CMP_BLOCK, NSA_HEAD_DIM), 0.1),
        "nsa_cmp_w1": nrm((NB, 2, CMP_BLOCK * NSA_HEAD_DIM, CMP_HIDDEN), (CMP_BLOCK * NSA_HEAD_DIM) ** -0.5),
        "nsa_cmp_w2": nrm((NB, 2, CMP_HIDDEN, NSA_HEAD_DIM), CMP_HIDDEN ** -0.5),
        "nsa_q_norm": 1.0 + nrm((NB, NSA_HEAD_DIM), 0.05),
        "nsa_k_norm": 1.0 + nrm((NB, 3, NSA_HEAD_DIM), 0.05),
        "nsa_w_out": nrm((NB, D, D), 0.5 * D ** -0.5),
    }


def reference(x, mix_norm, mlp_norm, mlp_w1, mlp_w2,
              rwkv_mu, rwkv_w_in, rwkv_w0, rwkv_w1, rwkv_w2, rwkv_a0, rwkv_a1, rwkv_a2,
              rwkv_v0, rwkv_v1, rwkv_v2, rwkv_g1, rwkv_g2, rwkv_k_k, rwkv_k_a, rwkv_r_k,
              rwkv_ln_w, rwkv_ln_b, rwkv_w_out,
              nsa_w_in, nsa_cmp_pe, nsa_cmp_w1, nsa_cmp_w2, nsa_q_norm, nsa_k_norm, nsa_w_out):
    h = x
    v_first = None
    for i in range(DEPTH):
        xn = rms_norm(h, mix_norm[i])
        j = i // N_MIXERS
        if i % N_MIXERS == 0:
            vres = None if j == 0 else (rwkv_v0[j - 1], rwkv_v1[j - 1], rwkv_v2[j - 1])
            y, v_raw = rwkv7_time_mix(xn, rwkv_mu[j], rwkv_w_in[j], rwkv_w0[j], rwkv_w1[j], rwkv_w2[j],
                                      rwkv_a0[j], rwkv_a1[j], rwkv_a2[j], rwkv_g1[j], rwkv_g2[j],
                                      rwkv_k_k[j], rwkv_k_a[j], rwkv_r_k[j], rwkv_ln_w[j], rwkv_ln_b[j],
                                      rwkv_w_out[j], v_first, vres)
            if j == 0:
                v_first = v_raw
        else:
            y = nsa_mix(xn, nsa_w_in[j], nsa_cmp_pe[j], nsa_cmp_w1[j], nsa_cmp_w2[j],
                        nsa_q_norm[j], nsa_k_norm[j], nsa_w_out[j])
        h = h + y
        h = h + squared_relu_mlp(rms_norm(h, mlp_norm[i]), mlp_w1[i], mlp_w2[i])
    return h
```

```python
import functools

import jax
import jax.numpy as jnp
import numpy as np
from jax import lax
from jax.experimental import pallas as pl
from jax.experimental.pallas import tpu as pltpu

F32 = jnp.float32
BF16 = jnp.bfloat16

D_MODEL = 1024
MLP_HIDDEN = 4 * D_MODEL
NORM_EPS = 1e-6
LANES = 128
VMEM_LIMIT = 56 * 1024 * 1024

RWKV_N = 64
RWKV_GN_EPS = 64e-5
WKV_CHUNK = 64
WKV_TBLOCK = 256

NSA_HKV = 4
NSA_G = 4
NSA_DH = 64
CMP_STRIDE = 16
CMP_HID = 128
SLC_BLOCK = 64
N_SELECT = 16
WINDOW = 512
SEL_BIG = 1e9
MASK_NEG = -1e30


def _dot(a, b):
    return jnp.dot(a, b, preferred_element_type=F32)


def _dot_nt(a, b):
    return lax.dot_general(a, b, (((1,), (1,)), ((), ())), preferred_element_type=F32)


def _bf(x):
    return x.astype(BF16)


def _split2(x):
    hi = x.astype(BF16)
    lo = (x - hi.astype(F32)).astype(BF16)
    return hi, lo


def _dot_hl(x, w):
    hi, lo = _split2(x)
    return _dot(hi, w) + _dot(lo, w)


def _iota(shape, dim):
    return lax.broadcasted_iota(jnp.int32, shape, dim)


def _bd64(n=LANES):
    return jnp.where((_iota((n, n), 0) >> 6) == (_iota((n, n), 1) >> 6), 1.0, 0.0).astype(BF16)


def _seg64_sum(x, bd):
    outs = [_dot_hl(x[:, c * LANES:(c + 1) * LANES], bd) for c in range(x.shape[1] // LANES)]
    return outs[0] if len(outs) == 1 else jnp.concatenate(outs, axis=1)


def _rms(x, g):
    return x * lax.rsqrt(jnp.mean(x * x, axis=-1, keepdims=True) + NORM_EPS) * g


def _const_spec(shape):
    return pl.BlockSpec(shape, lambda *_: (0,) * len(shape))


def _params(sem):
    return pltpu.CompilerParams(dimension_semantics=sem, vmem_limit_bytes=VMEM_LIMIT)


MLP_TM = 512
MLP_TH = 1024


def _out_mlp_kernel(h_ref, a_ref, wo_ref, g_ref, w1_ref, w2_ref, o_ref):
    h1 = h_ref[...] + _dot(a_ref[...], wo_ref[...])
    xb = _bf(_rms(h1, g_ref[...]))
    acc = h1
    for c in range(MLP_HIDDEN // MLP_TH):
        u = jnp.maximum(_dot(xb, w1_ref[:, c * MLP_TH:(c + 1) * MLP_TH]), 0.0)
        acc = acc + _dot(_bf(u * u), w2_ref[c * MLP_TH:(c + 1) * MLP_TH, :])
    o_ref[...] = acc


def _out_mlp(h, a, wo, g, w1, w2):
    m = h.shape[0]
    row = lambda i: (i, 0)
    return pl.pallas_call(
        _out_mlp_kernel,
        grid=(m // MLP_TM,),
        in_specs=[pl.BlockSpec((MLP_TM, D_MODEL), row), pl.BlockSpec((MLP_TM, D_MODEL), row),
                  _const_spec((D_MODEL, D_MODEL)), _const_spec((1, D_MODEL)),
                  _const_spec((D_MODEL, MLP_HIDDEN)), _const_spec((MLP_HIDDEN, D_MODEL))],
        out_specs=pl.BlockSpec((MLP_TM, D_MODEL), row),
        out_shape=jax.ShapeDtypeStruct((m, D_MODEL), F32),
        compiler_params=_params(("parallel",)),
    )(h, a, wo, g, w1, w2)


RW_TM = 256
SHIFT_ROWS = 8


def _rwkv_prep_kernel(*refs, tiles_per_seq, vres):
    if vres:
        (h_ref, hp_ref, gn_ref, mu_ref, win_ref, w0_ref, w1_ref, w2_ref, a0_ref, a1_ref, a2_ref,
         g1_ref, g2_ref, kk_ref, ka_ref, vf_ref, v0_ref, v1_ref, v2_ref,
         r_o, lw_o, k_o, v_o, an_o, bb_o, g_o) = refs
    else:
        (h_ref, hp_ref, gn_ref, mu_ref, win_ref, w0_ref, w1_ref, w2_ref, a0_ref, a1_ref, a2_ref,
         g1_ref, g2_ref, kk_ref, ka_ref,
         r_o, lw_o, k_o, v_o, an_o, bb_o, g_o) = refs
    gn = gn_ref[...]
    xn = _rms(h_ref[...], gn)
    first = (pl.program_id(0) % tiles_per_seq) == 0
    prev = _rms(hp_ref[SHIFT_ROWS - 1:SHIFT_ROWS, :], gn)
    prev = jnp.where(first, 0.0, prev)
    xs = pltpu.roll(xn, 1, axis=0)
    xs = jnp.where(_iota(xn.shape, 0) == 0, prev, xs)
    dx = xs - xn
    xr, xw, xk, xv, xa, xg = (xn + dx * mu_ref[i:i + 1, :] for i in range(6))
    r = _dot(_bf(xr), win_ref[:, 0:D_MODEL])
    k = _dot(_bf(xk), win_ref[:, D_MODEL:2 * D_MODEL])
    v = _dot(_bf(xv), win_ref[:, 2 * D_MODEL:3 * D_MODEL])
    if vres:
        lo = _dot(_bf(_dot(_bf(xv), v1_ref[...])), v2_ref[...])
        v = v + (vf_ref[...] - v) * jax.nn.sigmoid(v0_ref[...] + lo)
    z = w0_ref[...] + _dot(_bf(jnp.tanh(_dot(_bf(xw), w1_ref[...]))), w2_ref[...])
    softplus = jnp.maximum(-z, 0.0) + jnp.log(1.0 + jnp.exp(-jnp.abs(z)))
    lw = -jnp.exp(-softplus - 0.5)
    a = jax.nn.sigmoid(a0_ref[...] + _dot(_bf(_dot(_bf(xa), a1_ref[...])), a2_ref[...]))
    g = _dot(_bf(jax.nn.sigmoid(_dot(_bf(xg), g1_ref[...]))), g2_ref[...])
    kk = k * kk_ref[...]
    nrm = jnp.sqrt(_seg64_sum(kk * kk, _bd64()))
    kk = kk / jnp.maximum(nrm, 1e-12)
    r_o[...] = r
    lw_o[...] = lw
    k_o[...] = k * (1.0 + (a - 1.0) * ka_ref[...])
    v_o[...] = v
    an_o[...] = -kk
    bb_o[...] = kk * a
    g_o[...] = g


def _rwkv_prep(h, seq, gn, mu, win, w0, w1, w2, a0, a1, a2, g1, g2, k_k, k_a, vres):
    m = h.shape[0]
    row = lambda i: (i, 0)
    prev = lambda i: (jnp.maximum(i * (RW_TM // SHIFT_ROWS) - 1, 0), 0)
    vec = _const_spec((1, D_MODEL))
    full = lambda a: _const_spec(a.shape)
    args = [h, h, gn, mu, win, w0, w1, w2, a0, a1, a2, g1, g2, k_k, k_a]
    specs = [pl.BlockSpec((RW_TM, D_MODEL), row), pl.BlockSpec((SHIFT_ROWS, D_MODEL), prev), vec, full(mu),
             full(win), vec, full(w1), full(w2), vec, full(a1), full(a2), full(g1), full(g2), vec, vec]
    if vres is not None:
        v_first, v0, v1, v2 = vres
        args += [v_first, v0, v1, v2]
        specs += [pl.BlockSpec((RW_TM, D_MODEL), row), vec, full(v1), full(v2)]
    out = jax.ShapeDtypeStruct((m, D_MODEL), F32)
    return pl.pallas_call(
        functools.partial(_rwkv_prep_kernel, tiles_per_seq=seq // RW_TM, vres=vres is not None),
        grid=(m // RW_TM,),
        in_specs=specs,
        out_specs=[pl.BlockSpec((RW_TM, D_MODEL), row)] * 7,
        out_shape=[out] * 7,
        compiler_params=_params(("parallel",)),
    )(*args)


def _wkv_kernel(r_ref, lw_ref, k_ref, v_ref, a_ref, b_ref, g_ref, rk_ref, lnw_ref, lnb_ref,
                o_ref, s_ref, y_ref):
    L = WKV_CHUNK
    L2 = 2 * L

    @pl.when(pl.program_id(2) == 0)
    def _():
        s_ref[...] = jnp.zeros_like(s_ref)

    head0 = _iota((1, LANES), 1) < RWKV_N
    ii = _iota((L2, L2), 0)
    jj = _iota((L2, L2), 1)
    same = (ii >= L) == (jj >= L)
    tril_strict = same & (jj < ii)
    tril_incl = same & (jj <= ii)
    eye = jnp.where(ii == jj, 1.0, 0.0)
    cum_mat = jnp.where(_iota((L, L), 1) <= _iota((L, L), 0), 1.0, 0.0).astype(BF16)
    bd = (_iota((LANES, LANES), 0) >> 6) == (_iota((LANES, LANES), 1) >> 6)

    def by_head(x):
        return jnp.concatenate([jnp.where(head0, x, 0.0), jnp.where(head0, 0.0, x)], axis=0)

    def twice(x):
        return jnp.concatenate([x, x], axis=0)

    def pick(x2):
        return jnp.where(head0, x2[:L], x2[L:])

    for c in range(WKV_TBLOCK // L):
        sl = pl.ds(c * L, L)
        lw = lw_ref[sl, :]
        hi = lw.astype(BF16)
        rem = lw - hi.astype(F32)
        mid = rem.astype(BF16)
        low = (rem - mid.astype(F32)).astype(BF16)
        cum = _dot(cum_mat, hi) + _dot(cum_mat, mid) + _dot(cum_mat, low)
        g_in = jnp.exp(cum)
        g_inv = jnp.exp(-cum)
        cum_l = cum[L - 1:L, :]
        g_to_end = jnp.exp(cum_l - cum)
        r = r_ref[sl, :]
        k = k_ref[sl, :]
        v = v_ref[sl, :]
        a = a_ref[sl, :]
        b = b_ref[sl, :]
        at = a * jnp.exp(cum - lw)
        rt = r * g_in
        a2 = _bf(by_head(at))
        r2 = _bf(by_head(rt))
        b2 = _bf(twice(b * g_inv))
        k2 = _bf(twice(k * g_inv))
        m_ab = jnp.where(tril_strict, _dot_nt(a2, b2), 0.0)
        m_ak = jnp.where(tril_strict, _dot_nt(a2, k2), 0.0)
        n_rb = jnp.where(tril_incl, _dot_nt(r2, b2), 0.0)
        n_rk = jnp.where(tril_incl, _dot_nt(r2, k2), 0.0)
        inv = eye + m_ab
        pw = m_ab
        for _ in range(int(np.log2(L)) - 1):
            pw = _dot(_bf(pw), _bf(pw))
            inv = inv + _dot(_bf(inv), _bf(pw))
        s = s_ref[...]
        ps = _dot_nt(_bf(jnp.concatenate([at, rt], axis=0)), _bf(s))
        v2 = _bf(twice(v))
        x2 = twice(ps[:L]) + _dot(_bf(m_ak), v2)
        u = pick(_dot(_bf(inv), _bf(x2)))
        y2 = twice(ps[L:]) + _dot(_bf(n_rb), _bf(twice(u))) + _dot(_bf(n_rk), v2)
        y_ref[sl, :] = pick(y2)
        uv = jnp.concatenate([u, v], axis=0)
        bk = jnp.concatenate([b * g_to_end, k * g_to_end], axis=0)
        upd = _dot(_bf(uv.T), _bf(bk))
        s_ref[...] = s * jnp.exp(cum_l) + jnp.where(bd, upd, 0.0)

    ones = jnp.where(bd, 1.0, 0.0).astype(BF16)
    y = y_ref[...]
    mean = _dot_hl(y, ones) * (1.0 / RWKV_N)
    d = y - mean
    var = _dot_hl(d * d, ones) * (1.0 / RWKV_N)
    yn = d * lax.rsqrt(var + RWKV_GN_EPS) * lnw_ref[...] + lnb_ref[...]
    v = v_ref[...]
    bonus = _dot_hl(r_ref[...] * k_ref[...] * rk_ref[...], ones) * v
    o_ref[...] = _bf((yn + bonus) * g_ref[...])


def _wkv(batch, seq, r, lw, k, v, an, bb, g, r_k, ln_w, ln_b):
    m = r.shape[0]
    nt = seq // WKV_TBLOCK
    blk = pl.BlockSpec((WKV_TBLOCK, LANES), lambda b, p, i: (b * nt + i, p))
    vec = pl.BlockSpec((1, LANES), lambda b, p, i: (0, p))
    return pl.pallas_call(
        _wkv_kernel,
        grid=(batch, D_MODEL // LANES, nt),
        in_specs=[blk] * 7 + [vec] * 3,
        out_specs=blk,
        out_shape=jax.ShapeDtypeStruct((m, D_MODEL), BF16),
        scratch_shapes=[pltpu.VMEM((LANES, LANES), F32), pltpu.VMEM((WKV_TBLOCK, LANES), F32)],
        compiler_params=_params(("parallel", "parallel", "arbitrary")),
    )(r, lw, k, v, an, bb, g, r_k, ln_w, ln_b)


NSA_TM = 256
KV_COLS = 6 * NSA_HKV * NSA_DH
KV_HEAD_COLS = 6 * NSA_DH
GATE_COLS = NSA_HKV * LANES


def _nsa_proj_kernel(h_ref, gn_ref, wq_ref, wkv_ref, wg_ref, qg_ref, kg_ref, km_ref, q_o, kv_o, gt_o):
    xb = _bf(_rms(h_ref[...], gn_ref[...]))
    bd = _bd64()
    inv_dh = 1.0 / NSA_DH
    q = _dot(xb, wq_ref[...])
    q_o[...] = q * lax.rsqrt(_seg64_sum(q * q, bd) * inv_dh + NORM_EPS) * qg_ref[...]
    kv = _dot(xb, wkv_ref[...])
    kvn = kv * lax.rsqrt(_seg64_sum(kv * kv, bd) * inv_dh + NORM_EPS) * kg_ref[...]
    kv_o[...] = jnp.where(km_ref[...] > 0.5, kvn, kv)
    gt_o[...] = jax.nn.sigmoid(_dot(xb, wg_ref[...]))


def _nsa_proj(h, gn, wq, wkv, wg, qg, kg, km):
    m = h.shape[0]
    row = lambda i: (i, 0)
    return pl.pallas_call(
        _nsa_proj_kernel,
        grid=(m // NSA_TM,),
        in_specs=[pl.BlockSpec((NSA_TM, D_MODEL), row), _const_spec((1, D_MODEL)),
                  _const_spec(wq.shape), _const_spec(wkv.shape), _const_spec(wg.shape),
                  _const_spec((1, D_MODEL)), _const_spec((1, KV_COLS)), _const_spec((1, KV_COLS))],
        out_specs=[pl.BlockSpec((NSA_TM, D_MODEL), row), pl.BlockSpec((NSA_TM, KV_COLS), row),
                   pl.BlockSpec((NSA_TM, GATE_COLS), row)],
        out_shape=[jax.ShapeDtypeStruct((m, D_MODEL), F32), jax.ShapeDtypeStruct((m, KV_COLS), F32),
                   jax.ShapeDtypeStruct((m, GATE_COLS), F32)],
        compiler_params=_params(("parallel",)),
    )(h, gn, wq, wkv, wg, qg, kg, km)


def _nsa_cmp_kernel(kv_ref, wa_ref, wb_ref, pa_ref, pb_ref, w2_ref, kg_ref, o_ref, *, n_blk):
    nb = n_blk + 1
    first = jnp.zeros((nb, 2 * CMP_HID), F32)
    second = jnp.zeros((nb, 2 * CMP_HID), F32)
    for p in range(CMP_STRIDE):
        x = kv_ref[pl.ds(p, nb, stride=CMP_STRIDE), :]
        first = first + _dot(_bf(x + pa_ref[p:p + 1, :]), wa_ref[p])
        second = second + _dot(_bf(x + pb_ref[p:p + 1, :]), wb_ref[p])
    hid = jax.nn.gelu(first + pltpu.roll(second, nb - 1, axis=0))
    out = _dot(_bf(hid), w2_ref[...])
    is_k = _iota((1, LANES), 1) < NSA_DH
    ss = _seg64_sum(out * out, _bd64()) * (1.0 / NSA_DH)
    o_ref[...] = jnp.where(is_k, out * lax.rsqrt(ss + NORM_EPS) * kg_ref[...], out)


def _nsa_cmp(batch, seq, kv, wa, wb, pa, pb, w2, kg):
    n_blk = seq // CMP_STRIDE - 1
    nb = n_blk + 1
    return pl.pallas_call(
        functools.partial(_nsa_cmp_kernel, n_blk=n_blk),
        grid=(batch, NSA_HKV),
        in_specs=[pl.BlockSpec((seq, LANES), lambda b, h: (b, h * (KV_HEAD_COLS // LANES))),
                  _const_spec(wa.shape), _const_spec(wb.shape), _const_spec(pa.shape), _const_spec(pb.shape),
                  _const_spec(w2.shape), _const_spec(kg.shape)],
        out_specs=pl.BlockSpec((None, None, nb, LANES), lambda b, h: (b, h, 0, 0)),
        out_shape=jax.ShapeDtypeStruct((batch, NSA_HKV, nb, LANES), F32),
        compiler_params=_params(("parallel", "parallel")),
    )(kv, wa, wb, pa, pb, w2, kg)


ATT_TQ = 128
ATT_TK = 128


def _nsa_attn_kernel(slope_ref, q_ref, kv_ref, cmp_ref, gt_ref, o_ref, m_s, l_s, acc_s, *, n_blk, n_sel):
    tq, tk = ATT_TQ, ATT_TK
    R = NSA_G * tq
    hkv = pl.program_id(1)
    q0 = pl.program_id(2) * tq
    scale = NSA_DH ** -0.5
    q = q_ref[...]
    qs = jnp.concatenate([q[:, g * NSA_DH:(g + 1) * NSA_DH] for g in range(NSA_G)], axis=0) * scale
    qb = _bf(qs)
    t_one = q0 + _iota((tq, 1), 0)
    tpos = jnp.concatenate([t_one] * NSA_G, axis=0)
    slope = jnp.concatenate(
        [jnp.full((tq, 1), slope_ref[hkv * NSA_G + g], F32) for g in range(NSA_G)], axis=0)

    cmp = cmp_ref[...]
    nb = cmp.shape[0]
    kc = _bf(cmp[:, 0:NSA_DH])
    vc = _bf(cmp[:, NSA_DH:2 * NSA_DH])
    n_idx = _iota((1, nb), 1)
    dist = tpos - (n_idx * CMP_STRIDE + (2 * CMP_STRIDE - 1))
    valid = (dist >= 0) & (n_idx < n_blk)
    s = _dot_nt(qb, kc) - slope * dist.astype(F32)
    s = jnp.where(valid, s, -jnp.inf)
    mx = jnp.max(s, axis=-1, keepdims=True)
    mx = jnp.where(mx == -jnp.inf, 0.0, mx)
    e = jnp.where(valid, jnp.exp(s - mx), 0.0)
    p = e / jnp.maximum(jnp.sum(e, axis=-1, keepdims=True), 1e-30)
    o_cmp = _dot(_bf(p), vc)

    psum = p[0:tq] + p[tq:2 * tq] + p[2 * tq:3 * tq] + p[3 * tq:4 * tq]
    n_slc = (n_blk + 1) * CMP_STRIDE // SLC_BLOCK
    cn = _iota((nb, n_slc), 0) * CMP_STRIDE
    sj = _iota((nb, n_slc), 1) * SLC_BLOCK
    overlap = jnp.where((cn <= sj + SLC_BLOCK - 1) & (cn + 2 * CMP_STRIDE - 1 >= sj), 1.0, 0.0).astype(BF16)
    imp = _dot_hl(psum, overlap)
    blk = _iota((1, n_slc), 1)
    cur = t_one >> 6
    forced = (blk == 0) | (blk == cur) | (blk == cur - 1)
    imp = jnp.where(forced, SEL_BIG, imp)
    imp = jnp.where(blk * SLC_BLOCK <= t_one, imp, -SEL_BIG)
    rank = jnp.zeros((tq, n_slc), F32)
    for j in range(n_slc):
        col = imp[:, j:j + 1]
        ahead = (col > imp) | ((col == imp) & (blk > j))
        rank = rank + jnp.where(ahead, 1.0, 0.0)
    sel = _bf(jnp.where(rank < n_sel, 1.0, 0.0))

    def attend(lane0, k_lo, k_hi, use_sel):
        m_s[...] = jnp.full_like(m_s, MASK_NEG)
        l_s[...] = jnp.zeros_like(l_s)
        acc_s[...] = jnp.zeros_like(acc_s)

        def body(kt, carry):
            k0 = pl.multiple_of(kt * tk, tk)
            kvb = kv_ref[pl.ds(k0, tk), lane0:lane0 + LANES]
            kb = _bf(kvb[:, 0:NSA_DH])
            vb = _bf(kvb[:, NSA_DH:2 * NSA_DH])
            kpos = k0 + _iota((1, tk), 1)
            dist = tpos - kpos
            s = _dot_nt(qb, kb) - slope * dist.astype(F32)
            if use_sel:
                expand = jnp.where(_iota((n_slc, tk), 0) == ((k0 + _iota((n_slc, tk), 1)) >> 6), 1.0, 0.0)
                hit = _dot(sel, _bf(expand))
                ok = (dist >= 0) & (jnp.concatenate([hit] * NSA_G, axis=0) > 0.5)
            else:
                ok = (dist >= 0) & (dist < WINDOW)
            s = jnp.where(ok, s, MASK_NEG)
            m_old = m_s[...]
            m_new = jnp.maximum(m_old, jnp.max(s, axis=-1, keepdims=True))
            alpha = jnp.exp(m_old - m_new)
            pe = jnp.exp(s - m_new)
            l_s[...] = alpha * l_s[...] + jnp.sum(pe, axis=-1, keepdims=True)
            acc_s[...] = alpha * acc_s[...] + _dot(_bf(pe), vb)
            m_s[...] = m_new
            return carry

        lax.fori_loop(k_lo, k_hi, body, 0)
        return acc_s[...] / l_s[...]

    k_hi = (q0 + tq) // tk
    o_sel = attend(LANES, 0, k_hi, True)
    o_win = attend(2 * LANES, jnp.maximum(q0 - (WINDOW - 1), 0) // tk, k_hi, False)

    gt = gt_ref[...]
    outs = []
    for g in range(NSA_G):
        rows = slice(g * tq, (g + 1) * tq)
        outs.append(gt[:, 3 * g:3 * g + 1] * o_cmp[rows] + gt[:, 3 * g + 1:3 * g + 2] * o_sel[rows]
                    + gt[:, 3 * g + 2:3 * g + 3] * o_win[rows])
    o_ref[...] = _bf(jnp.concatenate(outs, axis=1))


def _nsa_attn(batch, seq, slopes, q, kv, cmp, gt):
    m = q.shape[0]
    nq = seq // ATT_TQ
    n_blk = seq // CMP_STRIDE - 1
    n_sel = min(N_SELECT, seq // SLC_BLOCK)
    R = NSA_G * ATT_TQ
    qcols = NSA_G * NSA_DH
    grid_spec = pltpu.PrefetchScalarGridSpec(
        num_scalar_prefetch=1,
        grid=(batch, NSA_HKV, nq),
        in_specs=[pl.BlockSpec((ATT_TQ, qcols), lambda b, h, i, s: (b * nq + i, h)),
                  pl.BlockSpec((seq, KV_HEAD_COLS), lambda b, h, i, s: (b, h)),
                  pl.BlockSpec((None, None, n_blk + 1, LANES), lambda b, h, i, s: (b, h, 0, 0)),
                  pl.BlockSpec((ATT_TQ, LANES), lambda b, h, i, s: (b * nq + i, h))],
        out_specs=pl.BlockSpec((ATT_TQ, qcols), lambda b, h, i, s: (b * nq + i, h)),
        scratch_shapes=[pltpu.VMEM((R, 1), F32), pltpu.VMEM((R, 1), F32), pltpu.VMEM((R, NSA_DH), F32)],
    )
    return pl.pallas_call(
        functools.partial(_nsa_attn_kernel, n_blk=n_blk, n_sel=n_sel),
        grid_spec=grid_spec,
        out_shape=jax.ShapeDtypeStruct((m, D_MODEL), BF16),
        compiler_params=_params(("parallel", "parallel", "arbitrary")),
    )(slopes, q, kv, cmp, gt)


def _pad_cols(w, n):
    return jnp.pad(w, ((0, 0), (0, n - w.shape[1])))


def _pad_rows(w, n):
    return jnp.pad(w, ((0, n - w.shape[0]), (0, 0)))


def _lora(w_down, w_up):
    rank = -(-w_down.shape[1] // LANES) * LANES
    return _bf(_pad_cols(w_down, rank)), _bf(_pad_rows(w_up, rank))


def _rwkv_layer(h, batch, seq, gn, mu, w_in, w0, w1, w2, a0, a1, a2, g1, g2, k_k, k_a, r_k, ln_w, ln_b,
                v_first, vres):
    vec = lambda x: x.reshape(1, D_MODEL)
    w1p, w2p = _lora(w1, w2)
    a1p, a2p = _lora(a1, a2)
    g1p, g2p = _lora(g1, g2)
    if vres is not None:
        v0, v1, v2 = vres
        v1p, v2p = _lora(v1, v2)
        vres = (v_first, vec(v0), v1p, v2p)
    mu8 = jnp.pad(mu, ((0, 2), (0, 0)))
    r, lw, k, v, an, bb, g = _rwkv_prep(h, seq, vec(gn), mu8, _bf(w_in), vec(w0), w1p, w2p, vec(a0), a1p, a2p,
                                        g1p, g2p, vec(k_k), vec(k_a), vres)
    y = _wkv(batch, seq, r, lw, k, v, an, bb, g, vec(r_k), vec(ln_w), vec(ln_b))
    return y, v


def _nsa_layer(h, batch, seq, gn, w_in, cmp_pe, cmp_w1, cmp_w2, q_norm, k_norm):
    hkv, dh = NSA_HKV, NSA_DH
    nq = D_MODEL
    wq = _bf(w_in[:, :nq])
    wkv = w_in[:, nq:nq + KV_COLS].reshape(D_MODEL, 6, hkv, dh).transpose(0, 2, 1, 3).reshape(D_MODEL, KV_COLS)
    wg = w_in[:, nq + KV_COLS:].reshape(D_MODEL, hkv, 3 * NSA_G)
    wg = jnp.pad(wg, ((0, 0), (0, 0), (0, LANES - 3 * NSA_G))).reshape(D_MODEL, GATE_COLS)
    qg = jnp.tile(q_norm, D_MODEL // dh).reshape(1, D_MODEL)
    one = jnp.ones((dh,), F32)
    zero = jnp.zeros((dh,), F32)
    kg = jnp.tile(jnp.concatenate([one, one, k_norm[1], one, k_norm[2], one]), hkv).reshape(1, KV_COLS)
    km = jnp.tile(jnp.concatenate([zero, zero, one, zero, one, zero]), hkv).reshape(1, KV_COLS)
    q, kv, gt = _nsa_proj(h, gn.reshape(1, D_MODEL), wq, _bf(wkv), _bf(wg), qg, kg, km)

    w1 = cmp_w1.reshape(2, 2 * CMP_STRIDE, dh, CMP_HID)
    zeros = jnp.zeros((2 * CMP_STRIDE, dh, CMP_HID), F32)
    wfull = jnp.concatenate([jnp.concatenate([w1[0], zeros], axis=2),
                             jnp.concatenate([zeros, w1[1]], axis=2)], axis=1)
    pe = jnp.concatenate([cmp_pe[0], cmp_pe[1]], axis=1)
    z2 = jnp.zeros((CMP_HID, dh), F32)
    w2 = jnp.concatenate([jnp.concatenate([cmp_w2[0], z2], axis=1),
                          jnp.concatenate([z2, cmp_w2[1]], axis=1)], axis=0)
    kg0 = jnp.concatenate([k_norm[0], one]).reshape(1, LANES)
    cmp = _nsa_cmp(batch, seq, kv, _bf(wfull[:CMP_STRIDE]), _bf(wfull[CMP_STRIDE:]),
                   pe[:CMP_STRIDE], pe[CMP_STRIDE:], _bf(w2), kg0)

    n_heads = hkv * NSA_G
    slopes = jnp.exp2(-8.0 * (jnp.arange(n_heads, dtype=F32) + 1.0) / n_heads)
    return _nsa_attn(batch, seq, slopes, q, kv, cmp, gt)


def kernel(x, mix_norm, mlp_norm, mlp_w1, mlp_w2, rwkv_mu, rwkv_w_in, rwkv_w0, rwkv_w1, rwkv_w2, rwkv_a0, rwkv_a1, rwkv_a2, rwkv_v0, rwkv_v1, rwkv_v2, rwkv_g1, rwkv_g2, rwkv_k_k, rwkv_k_a, rwkv_r_k, rwkv_ln_w, rwkv_ln_b, rwkv_w_out, nsa_w_in, nsa_cmp_pe, nsa_cmp_w1, nsa_cmp_w2, nsa_q_norm, nsa_k_norm, nsa_w_out):
    batch, seq, d = x.shape
    depth = mix_norm.shape[0]
    h = x.reshape(batch * seq, d)
    v_first = None
    for i in range(depth):
        j = i // 2
        if i % 2 == 0:
            vres = None if j == 0 else (rwkv_v0[j - 1], rwkv_v1[j - 1], rwkv_v2[j - 1])
            a, v_raw = _rwkv_layer(h, batch, seq, mix_norm[i], rwkv_mu[j], rwkv_w_in[j], rwkv_w0[j], rwkv_w1[j],
                                   rwkv_w2[j], rwkv_a0[j], rwkv_a1[j], rwkv_a2[j], rwkv_g1[j], rwkv_g2[j],
                                   rwkv_k_k[j], rwkv_k_a[j], rwkv_r_k[j].reshape(-1), rwkv_ln_w[j], rwkv_ln_b[j],
                                   v_first, vres)
            if j == 0:
                v_first = v_raw
            wo = rwkv_w_out[j]
        else:
            a = _nsa_layer(h, batch, seq, mix_norm[i], nsa_w_in[j], nsa_cmp_pe[j], nsa_cmp_w1[j], nsa_cmp_w2[j],
                           nsa_q_norm[j], nsa_k_norm[j])
            wo = nsa_w_out[j]
        h = _out_mlp(h, a, _bf(wo), mlp_norm[i].reshape(1, d), _bf(mlp_w1[i]), _bf(mlp_w2[i]))
    return h.reshape(batch, seq, d)
```

```python
import functools

import jax
import jax.numpy as jnp
import numpy as np
from jax import lax
from jax.experimental import pallas as pl
from jax.experimental.pallas import tpu as pltpu

F32 = jnp.float32
BF16 = jnp.bfloat16

D_MODEL = 1024
MLP_HIDDEN = 4 * D_MODEL
NORM_EPS = 1e-6
LANES = 128
VMEM_LIMIT = 56 * 1024 * 1024

RWKV_N = 64
RWKV_GN_EPS = 64e-5
WKV_CHUNK = 64
WKV_TBLOCK = 256

NSA_HKV = 4
NSA_G = 4
NSA_DH = 64
CMP_STRIDE = 16
CMP_HID = 128
SLC_BLOCK = 64
N_SELECT = 16
WINDOW = 512
SEL_BIG = 1e9
MASK_NEG = -1e30


def _dot(a, b):
    return jnp.dot(a, b, preferred_element_type=F32)


def _dot_nt(a, b):
    return lax.dot_general(a, b, (((1,), (1,)), ((), ())), preferred_element_type=F32)


def _bf(x):
    return x.astype(BF16)


def _split2(x):
    hi = x.astype(BF16)
    lo = (x - hi.astype(F32)).astype(BF16)
    return hi, lo


def _dot_hl(x, w):
    hi, lo = _split2(x)
    return _dot(hi, w) + _dot(lo, w)


def _iota(shape, dim):
    return lax.broadcasted_iota(jnp.int32, shape, dim)


def _bd64(n=LANES):
    return jnp.where((_iota((n, n), 0) >> 6) == (_iota((n, n), 1) >> 6), 1.0, 0.0).astype(BF16)


def _seg64_sum(x, bd):
    outs = [_dot_hl(x[:, c * LANES:(c + 1) * LANES], bd) for c in range(x.shape[1] // LANES)]
    return outs[0] if len(outs) == 1 else jnp.concatenate(outs, axis=1)


def _rms(x, g):
    return x * lax.rsqrt(jnp.mean(x * x, axis=-1, keepdims=True) + NORM_EPS) * g


def _const_spec(shape):
    return pl.BlockSpec(shape, lambda *_: (0,) * len(shape))


def _params(sem):
    return pltpu.CompilerParams(dimension_semantics=sem, vmem_limit_bytes=VMEM_LIMIT)


MLP_TM = 512
MLP_TH = 1024


def _out_mlp_kernel(h_ref, a_ref, wo_ref, g_ref, w1_ref, w2_ref, o_ref):
    h1 = h_ref[...] + _dot(a_ref[...], wo_ref[...])
    xb = _bf(_rms(h1, g_ref[...]))
    acc = h1
    for c in range(MLP_HIDDEN // MLP_TH):
        u = jnp.maximum(_dot(xb, w1_ref[:, c * MLP_TH:(c + 1) * MLP_TH]), 0.0)
        acc = acc + _dot(_bf(u * u), w2_ref[c * MLP_TH:(c + 1) * MLP_TH, :])
    o_ref[...] = acc


def _out_mlp(h, a, wo, g, w1, w2):
    m = h.shape[0]
    row = lambda i: (i, 0)
    return pl.pallas_call(
        _out_mlp_kernel,
        grid=(m // MLP_TM,),
        in_specs=[pl.BlockSpec((MLP_TM, D_MODEL), row), pl.BlockSpec((MLP_TM, D_MODEL), row),
                  _const_spec((D_MODEL, D_MODEL)), _const_spec((1, D_MODEL)),
                  _const_spec((D_MODEL, MLP_HIDDEN)), _const_spec((MLP_HIDDEN, D_MODEL))],
        out_specs=pl.BlockSpec((MLP_TM, D_MODEL), row),
        out_shape=jax.ShapeDtypeStruct((m, D_MODEL), F32),
        compiler_params=_params(("parallel",)),
    )(h, a, wo, g, w1, w2)


RW_TM = 256
SHIFT_ROWS = 8


def _rwkv_prep_kernel(*refs, tiles_per_seq, vres):
    if vres:
        (h_ref, hp_ref, gn_ref, mu_ref, win_ref, w0_ref, w1_ref, w2_ref, a0_ref, a1_ref, a2_ref,
         g1_ref, g2_ref, kk_ref, ka_ref, vf_ref, v0_ref, v1_ref, v2_ref,
         r_o, lw_o, k_o, v_o, an_o, bb_o, g_o) = refs
    else:
        (h_ref, hp_ref, gn_ref, mu_ref, win_ref, w0_ref, w1_ref, w2_ref, a0_ref, a1_ref, a2_ref,
         g1_ref, g2_ref, kk_ref, ka_ref,
         r_o, lw_o, k_o, v_o, an_o, bb_o, g_o) = refs
    gn = gn_ref[...]
    xn = _rms(h_ref[...], gn)
    first = (pl.program_id(0) % tiles_per_seq) == 0
    prev = _rms(hp_ref[SHIFT_ROWS - 1:SHIFT_ROWS, :], gn)
    prev = jnp.where(first, 0.0, prev)
    xs = pltpu.roll(xn, 1, axis=0)
    xs = jnp.where(_iota(xn.shape, 0) == 0, prev, xs)
    dx = xs - xn
    xr, xw, xk, xv, xa, xg = (xn + dx * mu_ref[i:i + 1, :] for i in range(6))
    r = _dot(_bf(xr), win_ref[:, 0:D_MODEL])
    k = _dot(_bf(xk), win_ref[:, D_MODEL:2 * D_MODEL])
    v = _dot(_bf(xv), win_ref[:, 2 * D_MODEL:3 * D_MODEL])
    if vres:
        lo = _dot(_bf(_dot(_bf(xv), v1_ref[...])), v2_ref[...])
        v = v + (vf_ref[...] - v) * jax.nn.sigmoid(v0_ref[...] + lo)
    z = w0_ref[...] + _dot(_bf(jnp.tanh(_dot(_bf(xw), w1_ref[...]))), w2_ref[...])
    softplus = jnp.maximum(-z, 0.0) + jnp.log(1.0 + jnp.exp(-jnp.abs(z)))
    lw = -jnp.exp(-softplus - 0.5)
    a = jax.nn.sigmoid(a0_ref[...] + _dot(_bf(_dot(_bf(xa), a1_ref[...])), a2_ref[...]))
    g = _dot(_bf(jax.nn.sigmoid(_dot(_bf(xg), g1_ref[...]))), g2_ref[...])
    kk = k * kk_ref[...]
    nrm = jnp.sqrt(_seg64_sum(kk * kk, _bd64()))
    kk = kk / jnp.maximum(nrm, 1e-12)
    r_o[...] = r
    lw_o[...] = lw
    k_o[...] = k * (1.0 + (a - 1.0) * ka_ref[...])
    v_o[...] = v
    an_o[...] = -kk
    bb_o[...] = kk * a
    g_o[...] = g


def _rwkv_prep(h, seq, gn, mu, win, w0, w1, w2, a0, a1, a2, g1, g2, k_k, k_a, vres):
    m = h.shape[0]
    row = lambda i: (i, 0)
    prev = lambda i: (jnp.maximum(i * (RW_TM // SHIFT_ROWS) - 1, 0), 0)
    vec = _const_spec((1, D_MODEL))
    full = lambda a: _const_spec(a.shape)
    args = [h, h, gn, mu, win, w0, w1, w2, a0, a1, a2, g1, g2, k_k, k_a]
    specs = [pl.BlockSpec((RW_TM, D_MODEL), row), pl.BlockSpec((SHIFT_ROWS, D_MODEL), prev), vec, full(mu),
             full(win), vec, full(w1), full(w2), vec, full(a1), full(a2), full(g1), full(g2), vec, vec]
    if vres is not None:
        v_first, v0, v1, v2 = vres
        args += [v_first, v0, v1, v2]
        specs += [pl.BlockSpec((RW_TM, D_MODEL), row), vec, full(v1), full(v2)]
    out = jax.ShapeDtypeStruct((m, D_MODEL), F32)
    return pl.pallas_call(
        functools.partial(_rwkv_prep_kernel, tiles_per_seq=seq // RW_TM, vres=vres is not None),
        grid=(m // RW_TM,),
        in_specs=specs,
        out_specs=[pl.BlockSpec((RW_TM, D_MODEL), row)] * 7,
        out_shape=[out] * 7,
        compiler_params=_params(("parallel",)),
    )(*args)


def _wkv_kernel(r_ref, lw_ref, k_ref, v_ref, a_ref, b_ref, g_ref, rk_ref, lnw_ref, lnb_ref,
                o_ref, s_ref, y_ref):
    L = WKV_CHUNK
    L2 = 2 * L

    @pl.when(pl.program_id(2) == 0)
    def _():
        s_ref[...] = jnp.zeros_like(s_ref)

    head0 = _iota((1, LANES), 1) < RWKV_N
    ii = _iota((L2, L2), 0)
    jj = _iota((L2, L2), 1)
    same = (ii >= L) == (jj >= L)
    tril_strict = same & (jj < ii)
    tril_incl = same & (jj <= ii)
    eye = jnp.where(ii == jj, 1.0, 0.0)
    cum_mat = jnp.where(_iota((L, L), 1) <= _iota((L, L), 0), 1.0, 0.0).astype(BF16)
    bd = (_iota((LANES, LANES), 0) >> 6) == (_iota((LANES, LANES), 1) >> 6)

    def by_head(x):
        return jnp.concatenate([jnp.where(head0, x, 0.0), jnp.where(head0, 0.0, x)], axis=0)

    def twice(x):
        return jnp.concatenate([x, x], axis=0)

    def pick(x2):
        return jnp.where(head0, x2[:L], x2[L:])

    for c in range(WKV_TBLOCK // L):
        sl = pl.ds(c * L, L)
        lw = lw_ref[sl, :]
        hi = lw.astype(BF16)
        rem = lw - hi.astype(F32)
        mid = rem.astype(BF16)
        low = (rem - mid.astype(F32)).astype(BF16)
        cum = _dot(cum_mat, hi) + _dot(cum_mat, mid) + _dot(cum_mat, low)
        g_in = jnp.exp(cum)
        g_inv = jnp.exp(-cum)
        cum_l = cum[L - 1:L, :]
        g_to_end = jnp.exp(cum_l - cum)
        r = r_ref[sl, :]
        k = k_ref[sl, :]
        v = v_ref[sl, :]
        a = a_ref[sl, :]
        b = b_ref[sl, :]
        at = a * jnp.exp(cum - lw)
        rt = r * g_in
        a2 = _bf(by_head(at))
        r2 = _bf(by_head(rt))
        b2 = _bf(twice(b * g_inv))
        k2 = _bf(twice(k * g_inv))
        m_ab = jnp.where(tril_strict, _dot_nt(a2, b2), 0.0)
        m_ak = jnp.where(tril_strict, _dot_nt(a2, k2), 0.0)
        n_rb = jnp.where(tril_incl, _dot_nt(r2, b2), 0.0)
        n_rk = jnp.where(tril_incl, _dot_nt(r2, k2), 0.0)
        inv = eye + m_ab
        pw = m_ab
        for _ in range(int(np.log2(L)) - 1):
            pw = _dot(_bf(pw), _bf(pw))
            inv = inv + _dot(_bf(inv), _bf(pw))
        s = s_ref[...]
        ps = _dot_nt(_bf(jnp.concatenate([at, rt], axis=0)), _bf(s))
        v2 = _bf(twice(v))
        x2 = twice(ps[:L]) + _dot(_bf(m_ak), v2)
        u = pick(_dot(_bf(inv), _bf(x2)))
        y2 = twice(ps[L:]) + _dot(_bf(n_rb), _bf(twice(u))) + _dot(_bf(n_rk), v2)
        y_ref[sl, :] = pick(y2)
        uv = jnp.concatenate([u, v], axis=0)
        bk = jnp.concatenate([b * g_to_end, k * g_to_end], axis=0)
        upd = _dot(_bf(uv.T), _bf(bk))
        s_ref[...] = s * jnp.exp(cum_l) + jnp.where(bd, upd, 0.0)

    ones = jnp.where(bd, 1.0, 0.0).astype(BF16)
    y = y_ref[...]
    mean = _dot_hl(y, ones) * (1.0 / RWKV_N)
    d = y - mean
    var = _dot_hl(d * d, ones) * (1.0 / RWKV_N)
    yn = d * lax.rsqrt(var + RWKV_GN_EPS) * lnw_ref[...] + lnb_ref[...]
    v = v_ref[...]
    bonus = _dot_hl(r_ref[...] * k_ref[...] * rk_ref[...], ones) * v
    o_ref[...] = _bf((yn + bonus) * g_ref[...])


def _wkv(batch, seq, r, lw, k, v, an, bb, g, r_k, ln_w, ln_b):
    m = r.shape[0]
    nt = seq // WKV_TBLOCK
    blk = pl.BlockSpec((WKV_TBLOCK, LANES), lambda b, p, i: (b * nt + i, p))
    vec = pl.BlockSpec((1, LANES), lambda b, p, i: (0, p))
    return pl.pallas_call(
        _wkv_kernel,
        grid=(batch, D_MODEL // LANES, nt),
        in_specs=[blk] * 7 + [vec] * 3,
        out_specs=blk,
        out_shape=jax.ShapeDtypeStruct((m, D_MODEL), BF16),
        scratch_shapes=[pltpu.VMEM((LANES, LANES), F32), pltpu.VMEM((WKV_TBLOCK, LANES), F32)],
        compiler_params=_params(("parallel", "parallel", "arbitrary")),
    )(r, lw, k, v, an, bb, g, r_k, ln_w, ln_b)


ATT_TQ = 256
ATT_TK = 256
NSA_TM = ATT_TK
N_POS_LANES = 3
KVA_HEAD_COLS = 4 * LANES
KVA_COLS = NSA_HKV * KVA_HEAD_COLS
KVC_COLS = NSA_HKV * LANES
GATE_COLS = NSA_HKV * LANES
COL_PLAIN, COL_KEY_NORM, COL_POS_MOD, COL_POS_TILE, COL_ONE = range(5)


def _nsa_proj_kernel(h_ref, gn_ref, wq_ref, wkva_ref, wkvc_ref, wg_ref, qg_ref, kg_ref, km_ref,
                     q_o, kva_o, kvc_o, gt_o, *, tiles_per_seq):
    xb = _bf(_rms(h_ref[...], gn_ref[...]))
    bd = _bd64()
    inv_dh = 1.0 / NSA_DH
    q = _dot(xb, wq_ref[...])
    q_o[...] = _bf(q * lax.rsqrt(_seg64_sum(q * q, bd) * inv_dh + NORM_EPS) * qg_ref[...])
    kv = _dot(xb, wkva_ref[...])
    kvn = kv * lax.rsqrt(_seg64_sum(kv * kv, bd) * inv_dh + NORM_EPS) * kg_ref[...]
    kind = km_ref[...]
    pos_mod = _iota(kv.shape, 0).astype(F32)
    pos_tile = (pl.program_id(0) % tiles_per_seq).astype(F32)
    kv = jnp.where(kind == COL_KEY_NORM, kvn, kv)
    kv = jnp.where(kind == COL_POS_MOD, pos_mod, kv)
    kv = jnp.where(kind == COL_POS_TILE, pos_tile, kv)
    kva_o[...] = _bf(jnp.where(kind == COL_ONE, 1.0, kv))
    kvc_o[...] = _dot(xb, wkvc_ref[...])
    gt_o[...] = jax.nn.sigmoid(_dot(xb, wg_ref[...]))


def _nsa_proj(h, seq, gn, wq, wkva, wkvc, wg, qg, kg, km):
    m = h.shape[0]
    row = lambda i: (i, 0)
    return pl.pallas_call(
        functools.partial(_nsa_proj_kernel, tiles_per_seq=seq // NSA_TM),
        grid=(m // NSA_TM,),
        in_specs=[pl.BlockSpec((NSA_TM, D_MODEL), row), _const_spec((1, D_MODEL)),
                  _const_spec(wq.shape), _const_spec(wkva.shape), _const_spec(wkvc.shape), _const_spec(wg.shape),
                  _const_spec((1, D_MODEL)), _const_spec((1, KVA_COLS)), _const_spec((1, KVA_COLS))],
        out_specs=[pl.BlockSpec((NSA_TM, D_MODEL), row), pl.BlockSpec((NSA_TM, KVA_COLS), row),
                   pl.BlockSpec((NSA_TM, KVC_COLS), row), pl.BlockSpec((NSA_TM, GATE_COLS), row)],
        out_shape=[jax.ShapeDtypeStruct((m, D_MODEL), BF16), jax.ShapeDtypeStruct((m, KVA_COLS), BF16),
                   jax.ShapeDtypeStruct((m, KVC_COLS), F32), jax.ShapeDtypeStruct((m, GATE_COLS), F32)],
        compiler_params=_params(("parallel",)),
    )(h, gn, wq, wkva, wkvc, wg, qg, kg, km)


def _nsa_cmp_kernel(kv_ref, wa_ref, wb_ref, pa_ref, pb_ref, w2_ref, kg_ref, o_ref, *, n_blk):
    nb = n_blk + 1
    first = jnp.zeros((nb, 2 * CMP_HID), F32)
    second = jnp.zeros((nb, 2 * CMP_HID), F32)
    for p in range(CMP_STRIDE):
        x = kv_ref[pl.ds(p, nb, stride=CMP_STRIDE), :]
        first = first + _dot(_bf(x + pa_ref[p:p + 1, :]), wa_ref[p])
        second = second + _dot(_bf(x + pb_ref[p:p + 1, :]), wb_ref[p])
    hid = jax.nn.gelu(first + pltpu.roll(second, nb - 1, axis=0))
    out = _dot(_bf(hid), w2_ref[...])
    is_k = _iota((1, LANES), 1) < NSA_DH
    ss = _seg64_sum(out * out, _bd64()) * (1.0 / NSA_DH)
    o_ref[...] = jnp.where(is_k, out * lax.rsqrt(ss + NORM_EPS) * kg_ref[...], out)


def _nsa_cmp(batch, seq, kv, wa, wb, pa, pb, w2, kg):
    n_blk = seq // CMP_STRIDE - 1
    nb = n_blk + 1
    return pl.pallas_call(
        functools.partial(_nsa_cmp_kernel, n_blk=n_blk),
        grid=(batch, NSA_HKV),
        in_specs=[pl.BlockSpec((seq, LANES), lambda b, h: (b, h)),
                  _const_spec(wa.shape), _const_spec(wb.shape), _const_spec(pa.shape), _const_spec(pb.shape),
                  _const_spec(w2.shape), _const_spec(kg.shape)],
        out_specs=pl.BlockSpec((None, None, nb, LANES), lambda b, h: (b, h, 0, 0)),
        out_shape=jax.ShapeDtypeStruct((batch, NSA_HKV, nb, LANES), F32),
        compiler_params=_params(("parallel", "parallel")),
    )(kv, wa, wb, pa, pb, w2, kg)


def _nsa_attn_kernel(slope_ref, q_ref, kv_ref, oh_ref, cmp_ref, gt_ref, o_ref, qam_s, m_s, acc_s,
                     *, n_blk, n_sel, n_slc):
    tq, tk = ATT_TQ, ATT_TK
    R = NSA_G * tq
    hkv = pl.program_id(1)
    qi = pl.program_id(2)
    q0 = qi * tq
    q = q_ref[...].astype(F32) * (NSA_DH ** -0.5)
    qs = jnp.concatenate([q[:, g * NSA_DH:(g + 1) * NSA_DH] for g in range(NSA_G)], axis=0)
    t_one = q0 + _iota((tq, 1), 0)
    tpos = jnp.concatenate([t_one] * NSA_G, axis=0)
    slope = jnp.concatenate(
        [jnp.full((tq, 1), slope_ref[hkv * NSA_G + g], F32) for g in range(NSA_G)], axis=0)

    cmp = cmp_ref[...]
    nb = cmp.shape[0]
    kc = _bf(cmp[:, 0:NSA_DH])
    vc = _bf(cmp[:, NSA_DH:2 * NSA_DH])
    n_idx = _iota((1, nb), 1)
    dist = tpos - (n_idx * CMP_STRIDE + (2 * CMP_STRIDE - 1))
    valid = (dist >= 0) & (n_idx < n_blk)
    s = _dot_nt(_bf(qs), kc) - slope * dist.astype(F32)
    s = jnp.where(valid, s, -jnp.inf)
    mx = jnp.max(s, axis=-1, keepdims=True)
    mx = jnp.where(mx == -jnp.inf, 0.0, mx)
    e = jnp.where(valid, jnp.exp(s - mx), 0.0)
    p = e / jnp.maximum(jnp.sum(e, axis=-1, keepdims=True), 1e-30)
    o_cmp = _dot(_bf(p), vc)

    psum = p[0:tq] + p[tq:2 * tq] + p[2 * tq:3 * tq] + p[3 * tq:4 * tq]
    sj = _iota((n_slc, nb), 0) * SLC_BLOCK
    cn = _iota((n_slc, nb), 1) * CMP_STRIDE
    overlap_t = jnp.where((cn <= sj + SLC_BLOCK - 1) & (cn + 2 * CMP_STRIDE - 1 >= sj), 1.0, 0.0).astype(BF16)
    p_hi, p_lo = _split2(psum)
    imp = _dot_nt(overlap_t, p_hi) + _dot_nt(overlap_t, p_lo)
    blk = _iota((n_slc, 1), 0)
    ahead_of_cur = ((q0 + _iota((1, tq), 1)) >> 6) - blk
    forced = (blk * ahead_of_cur * (ahead_of_cur - 1)) == 0
    imp = jnp.where(forced, SEL_BIG, imp)
    imp = jnp.where(ahead_of_cur >= 0, imp, -SEL_BIG)
    sub = 8
    ranks = []
    for lo in range(0, n_slc, sub):
        mine = imp[lo:lo + sub, :]
        rk = jnp.zeros((sub, tq), F32)
        for j in range(n_slc):
            other = imp[j:j + 1, :]
            if j < lo:
                rk = rk + jnp.where(other >= mine, 1.0, 0.0)
            elif j >= lo + sub:
                rk = rk + jnp.where(other > mine, 1.0, 0.0)
            else:
                rk = rk + jnp.where(blk[lo:lo + sub] > j, jnp.where(other >= mine, 1.0, 0.0),
                                    jnp.where(other > mine, 1.0, 0.0))
        ranks.append(rk)
    rank = jnp.concatenate(ranks, axis=0)
    not_sel = jnp.where(rank < n_sel, 0.0, 1.0).T

    s1 = slope.astype(BF16).astype(F32)
    s2 = (slope - s1).astype(BF16).astype(F32)
    s3 = (slope - s1 - s2).astype(BF16).astype(F32)
    lane = _iota((R, LANES), 1)
    qa = jnp.concatenate([qs, jnp.zeros((R, LANES - NSA_DH), F32)], axis=1)
    for n, term in enumerate((s1, s2, s3)):
        qa = jnp.where(lane == NSA_DH + n, term, qa)
        qa = jnp.where(lane == NSA_DH + N_POS_LANES + n, term * float(tk), qa)
    qm = jnp.concatenate([jnp.concatenate([not_sel] * NSA_G, axis=0),
                          jnp.zeros((R, LANES - n_slc), F32)], axis=1)
    qam_s[...] = _bf(jnp.concatenate([qa, qm], axis=1))

    def reset():
        m_s[...] = jnp.full_like(m_s, MASK_NEG)
        acc_s[...] = jnp.zeros_like(acc_s)

    def online(g, s, vb):
        m_old = m_s[g]
        m_new = jnp.maximum(m_old, jnp.max(s, axis=-1, keepdims=True))
        pe = jnp.exp(s - jnp.concatenate([m_new] * (tk // LANES), axis=1))
        acc_s[g] = jnp.exp(m_old - m_new) * acc_s[g] + _dot(_bf(pe), vb)
        m_s[g] = m_new

    def sel_tile(kt, diagonal):
        k0 = pl.multiple_of(kt * tk, tk)
        rows = pl.ds(k0, tk)
        kaug = jnp.concatenate([kv_ref[rows, 0:LANES], oh_ref[rows, :]], axis=1)
        vb = kv_ref[rows, 2 * LANES:3 * LANES]
        for g in range(NSA_G):
            s = _dot_nt(qam_s[g * tq:(g + 1) * tq, :], kaug)
            if diagonal:
                s = jnp.where(k0 + _iota((1, tk), 1) <= t_one, s, MASK_NEG)
            online(g, s, vb)

    def sel_body(kt, carry):
        sel_tile(kt, False)
        return carry

    def win_body(kt, carry):
        k0 = pl.multiple_of(kt * tk, tk)
        rows = pl.ds(k0, tk)
        kw = kv_ref[rows, LANES:2 * LANES]
        vb = kv_ref[rows, 3 * LANES:4 * LANES]
        dist = t_one - (k0 + _iota((1, tk), 1))
        ok = (dist >= 0) & (dist < WINDOW)
        for g in range(NSA_G):
            s = _dot_nt(qam_s[g * tq:(g + 1) * tq, 0:LANES], kw)
            online(g, jnp.where(ok, s, MASK_NEG), vb)
        return carry

    def result():
        acc = acc_s[...]
        return acc[:, :, 0:NSA_DH] / acc[:, :, NSA_DH:NSA_DH + 1]

    reset()
    lax.fori_loop(0, qi, sel_body, 0)
    sel_tile(qi, True)
    o_sel = result()
    reset()
    lax.fori_loop(jnp.maximum(q0 - (WINDOW - 1), 0) // tk, qi + 1, win_body, 0)
    o_win = result()

    gt = gt_ref[...]
    outs = []
    for g in range(NSA_G):
        outs.append(gt[:, 3 * g:3 * g + 1] * o_cmp[g * tq:(g + 1) * tq] + gt[:, 3 * g + 1:3 * g + 2] * o_sel[g]
                    + gt[:, 3 * g + 2:3 * g + 3] * o_win[g])
    o_ref[...] = _bf(jnp.concatenate(outs, axis=1))


def _nsa_attn(batch, seq, slopes, q, kva, onehot, cmp, gt):
    assert ATT_TQ == ATT_TK
    m = q.shape[0]
    nq = seq // ATT_TQ
    n_blk = seq // CMP_STRIDE - 1
    n_slc = seq // SLC_BLOCK
    n_sel = min(N_SELECT, n_slc)
    R = NSA_G * ATT_TQ
    qcols = NSA_G * NSA_DH
    grid_spec = pltpu.PrefetchScalarGridSpec(
        num_scalar_prefetch=1,
        grid=(batch, NSA_HKV, nq),
        in_specs=[pl.BlockSpec((ATT_TQ, qcols), lambda b, h, i, s: (b * nq + i, h)),
                  pl.BlockSpec((seq, KVA_HEAD_COLS), lambda b, h, i, s: (b, h)),
                  pl.BlockSpec((seq, LANES), lambda b, h, i, s: (0, 0)),
                  pl.BlockSpec((None, None, n_blk + 1, LANES), lambda b, h, i, s: (b, h, 0, 0)),
                  pl.BlockSpec((ATT_TQ, LANES), lambda b, h, i, s: (b * nq + i, h))],
        out_specs=pl.BlockSpec((ATT_TQ, qcols), lambda b, h, i, s: (b * nq + i, h)),
        scratch_shapes=[pltpu.VMEM((R, 2 * LANES), BF16), pltpu.VMEM((NSA_G, ATT_TQ, LANES), F32),
                        pltpu.VMEM((NSA_G, ATT_TQ, LANES), F32)],
    )
    return pl.pallas_call(
        functools.partial(_nsa_attn_kernel, n_blk=n_blk, n_sel=n_sel, n_slc=n_slc),
        grid_spec=grid_spec,
        out_shape=jax.ShapeDtypeStruct((m, D_MODEL), BF16),
        compiler_params=_params(("parallel", "parallel", "arbitrary")),
    )(slopes, q, kva, onehot, cmp, gt)


def _pad_cols(w, n):
    return jnp.pad(w, ((0, 0), (0, n - w.shape[1])))


def _pad_rows(w, n):
    return jnp.pad(w, ((0, n - w.shape[0]), (0, 0)))


def _lora(w_down, w_up):
    rank = -(-w_down.shape[1] // LANES) * LANES
    return _bf(_pad_cols(w_down, rank)), _bf(_pad_rows(w_up, rank))


def _rwkv_layer(h, batch, seq, gn, mu, w_in, w0, w1, w2, a0, a1, a2, g1, g2, k_k, k_a, r_k, ln_w, ln_b,
                v_first, vres):
    vec = lambda x: x.reshape(1, D_MODEL)
    w1p, w2p = _lora(w1, w2)
    a1p, a2p = _lora(a1, a2)
    g1p, g2p = _lora(g1, g2)
    if vres is not None:
        v0, v1, v2 = vres
        v1p, v2p = _lora(v1, v2)
        vres = (v_first, vec(v0), v1p, v2p)
    mu8 = jnp.pad(mu, ((0, 2), (0, 0)))
    r, lw, k, v, an, bb, g = _rwkv_prep(h, seq, vec(gn), mu8, _bf(w_in), vec(w0), w1p, w2p, vec(a0), a1p, a2p,
                                        g1p, g2p, vec(k_k), vec(k_a), vres)
    y = _wkv(batch, seq, r, lw, k, v, an, bb, g, vec(r_k), vec(ln_w), vec(ln_b))
    return y, v


def _nsa_layer(h, batch, seq, gn, w_in, cmp_pe, cmp_w1, cmp_w2, q_norm, k_norm):
    hkv, dh = NSA_HKV, NSA_DH
    nq = D_MODEL
    kv_cols = hkv * dh
    wq = _bf(w_in[:, :nq])
    wkv = w_in[:, nq:nq + 6 * kv_cols].reshape(D_MODEL, 6, hkv, dh)
    zpad = jnp.zeros((D_MODEL, hkv, dh), F32)
    wkva = jnp.stack([wkv[:, 2], zpad, wkv[:, 4], zpad, wkv[:, 3], zpad, wkv[:, 5], zpad],
                     axis=2).reshape(D_MODEL, KVA_COLS)
    wkvc = jnp.stack([wkv[:, 0], wkv[:, 1]], axis=2).reshape(D_MODEL, KVC_COLS)
    wg = w_in[:, nq + 6 * kv_cols:].reshape(D_MODEL, hkv, 3 * NSA_G)
    wg = jnp.pad(wg, ((0, 0), (0, 0), (0, LANES - 3 * NSA_G))).reshape(D_MODEL, GATE_COLS)
    qg = jnp.tile(q_norm, D_MODEL // dh).reshape(1, D_MODEL)
    one = jnp.ones((dh,), F32)
    zero = jnp.zeros((dh,), F32)
    lane = jnp.arange(dh)
    pos_kind = jnp.where(lane < N_POS_LANES, COL_POS_MOD, jnp.where(lane < 2 * N_POS_LANES, COL_POS_TILE, COL_PLAIN))
    one_kind = jnp.where(lane == 0, COL_ONE, COL_PLAIN)
    key_kind = jnp.full((dh,), COL_KEY_NORM)
    plain = jnp.full((dh,), COL_PLAIN)
    kg = jnp.tile(jnp.concatenate([k_norm[1], one, k_norm[2], one, one, one, one, one]), hkv).reshape(1, KVA_COLS)
    km = jnp.tile(jnp.concatenate([key_kind, pos_kind, key_kind, pos_kind, plain, one_kind, plain, one_kind]),
                  hkv).reshape(1, KVA_COLS).astype(jnp.int32)
    q, kva, kvc, gt = _nsa_proj(h, seq, gn.reshape(1, D_MODEL), wq, _bf(wkva), _bf(wkvc), _bf(wg), qg, kg, km)

    w1 = cmp_w1.reshape(2, 2 * CMP_STRIDE, dh, CMP_HID)
    zeros = jnp.zeros((2 * CMP_STRIDE, dh, CMP_HID), F32)
    wfull = jnp.concatenate([jnp.concatenate([w1[0], zeros], axis=2),
                             jnp.concatenate([zeros, w1[1]], axis=2)], axis=1)
    pe = jnp.concatenate([cmp_pe[0], cmp_pe[1]], axis=1)
    z2 = jnp.zeros((CMP_HID, dh), F32)
    w2 = jnp.concatenate([jnp.concatenate([cmp_w2[0], z2], axis=1),
                          jnp.concatenate([z2, cmp_w2[1]], axis=1)], axis=0)
    kg0 = jnp.concatenate([k_norm[0], one]).reshape(1, LANES)
    cmp = _nsa_cmp(batch, seq, kvc, _bf(wfull[:CMP_STRIDE]), _bf(wfull[CMP_STRIDE:]),
                   pe[:CMP_STRIDE], pe[CMP_STRIDE:], _bf(w2), kg0)

    n_heads = hkv * NSA_G
    slopes = jnp.exp2(-8.0 * (jnp.arange(n_heads, dtype=F32) + 1.0) / n_heads)
    onehot = jnp.where((jnp.arange(seq) // SLC_BLOCK)[:, None] == jnp.arange(LANES)[None, :], MASK_NEG, 0.0)
    return _nsa_attn(batch, seq, slopes, q, kva, _bf(onehot), cmp, gt)


def kernel(x, mix_norm, mlp_norm, mlp_w1, mlp_w2, rwkv_mu, rwkv_w_in, rwkv_w0, rwkv_w1, rwkv_w2, rwkv_a0, rwkv_a1, rwkv_a2, rwkv_v0, rwkv_v1, rwkv_v2, rwkv_g1, rwkv_g2, rwkv_k_k, rwkv_k_a, rwkv_r_k, rwkv_ln_w, rwkv_ln_b, rwkv_w_out, nsa_w_in, nsa_cmp_pe, nsa_cmp_w1, nsa_cmp_w2, nsa_q_norm, nsa_k_norm, nsa_w_out):
    batch, seq, d = x.shape
    depth = mix_norm.shape[0]
    h = x.reshape(batch * seq, d)
    v_first = None
    for i in range(depth):
        j = i // 2
        if i % 2 == 0:
            vres = None if j == 0 else (rwkv_v0[j - 1], rwkv_v1[j - 1], rwkv_v2[j - 1])
            a, v_raw = _rwkv_layer(h, batch, seq, mix_norm[i], rwkv_mu[j], rwkv_w_in[j], rwkv_w0[j], rwkv_w1[j],
                                   rwkv_w2[j], rwkv_a0[j], rwkv_a1[j], rwkv_a2[j], rwkv_g1[j], rwkv_g2[j],
                                   rwkv_k_k[j], rwkv_k_a[j], rwkv_r_k[j].reshape(-1), rwkv_ln_w[j], rwkv_ln_b[j],
                                   v_first, vres)
            if j == 0:
                v_first = v_raw
            wo = rwkv_w_out[j]
        else:
            a = _nsa_layer(h, batch, seq, mix_norm[i], nsa_w_in[j], nsa_cmp_pe[j], nsa_cmp_w1[j], nsa_cmp_w2[j],
                           nsa_q_norm[j], nsa_k_norm[j])
            wo = nsa_w_out[j]
        h = _out_mlp(h, a, _bf(wo), mlp_norm[i].reshape(1, d), _bf(mlp_w1[i]), _bf(mlp_w2[i]))
    return h.reshape(batch, seq, d)
```

```python
import functools

import jax
import jax.numpy as jnp
import numpy as np
from jax import lax
from jax.experimental import pallas as pl
from jax.experimental.pallas import tpu as pltpu

F32 = jnp.float32
BF16 = jnp.bfloat16

D_MODEL = 1024
MLP_HIDDEN = 4 * D_MODEL
NORM_EPS = 1e-6
LANES = 128
VMEM_LIMIT = 56 * 1024 * 1024

RWKV_N = 64
RWKV_GN_EPS = 64e-5
WKV_CHUNK = 64
WKV_TBLOCK = 128
WKV_PAIRS = 8

NSA_HKV = 4
NSA_G = 4
NSA_DH = 64
CMP_STRIDE = 16
CMP_HID = 128
SLC_BLOCK = 64
N_SELECT = 16
WINDOW = 512
SEL_BIG = 1e9
MASK_NEG = -1e30
LOG2_E = 1.4426950408889634
Q_PRESCALE = NSA_DH ** -0.5 * LOG2_E
PAD_TILE_LANE = LANES - 1


def _dot(a, b):
    return jnp.dot(a, b, preferred_element_type=F32)


def _dot_nt(a, b):
    return lax.dot_general(a, b, (((1,), (1,)), ((), ())), preferred_element_type=F32)


def _bf(x):
    return x.astype(BF16)


def _split2(x):
    hi = x.astype(BF16)
    lo = (x - hi.astype(F32)).astype(BF16)
    return hi, lo


def _dot_hl(x, w):
    hi, lo = _split2(x)
    return _dot(hi, w) + _dot(lo, w)


def _iota(shape, dim):
    return lax.broadcasted_iota(jnp.int32, shape, dim)


def _bd64(n=LANES):
    return jnp.where((_iota((n, n), 0) >> 6) == (_iota((n, n), 1) >> 6), 1.0, 0.0).astype(BF16)


def _seg64_sum(x, bd):
    outs = [_dot_hl(x[:, c * LANES:(c + 1) * LANES], bd) for c in range(x.shape[1] // LANES)]
    return outs[0] if len(outs) == 1 else jnp.concatenate(outs, axis=1)


def _rms(x, g):
    return x * lax.rsqrt(jnp.mean(x * x, axis=-1, keepdims=True) + NORM_EPS) * g


def _const_spec(shape):
    return pl.BlockSpec(shape, lambda *_: (0,) * len(shape))


def _params(sem):
    return pltpu.CompilerParams(dimension_semantics=sem, vmem_limit_bytes=VMEM_LIMIT)


MLP_TM = 512
MLP_TH = 1024


def _out_mlp_kernel(h_ref, a_ref, wo_ref, g_ref, w1_ref, w2_ref, o_ref):
    h1 = h_ref[...] + _dot(a_ref[...], wo_ref[...])
    xb = _bf(_rms(h1, g_ref[...]))
    acc = h1
    for c in range(MLP_HIDDEN // MLP_TH):
        u = jnp.maximum(_dot(xb, w1_ref[:, c * MLP_TH:(c + 1) * MLP_TH]), 0.0)
        acc = acc + _dot(_bf(u * u), w2_ref[c * MLP_TH:(c + 1) * MLP_TH, :])
    o_ref[...] = acc


def _out_mlp(h, a, wo, g, w1, w2):
    m = h.shape[0]
    row = lambda i: (i, 0)
    return pl.pallas_call(
        _out_mlp_kernel,
        grid=(m // MLP_TM,),
        in_specs=[pl.BlockSpec((MLP_TM, D_MODEL), row), pl.BlockSpec((MLP_TM, D_MODEL), row),
                  _const_spec((D_MODEL, D_MODEL)), _const_spec((1, D_MODEL)),
                  _const_spec((D_MODEL, MLP_HIDDEN)), _const_spec((MLP_HIDDEN, D_MODEL))],
        out_specs=pl.BlockSpec((MLP_TM, D_MODEL), row),
        out_shape=jax.ShapeDtypeStruct((m, D_MODEL), F32),
        compiler_params=_params(("parallel",)),
    )(h, a, wo, g, w1, w2)


RW_TM = 256
SHIFT_ROWS = 8


def _rwkv_prep_kernel(*refs, tiles_per_seq, vres):
    if vres:
        (h_ref, hp_ref, gn_ref, mu_ref, win_ref, w0_ref, w1_ref, w2_ref, a0_ref, a1_ref, a2_ref,
         g1_ref, g2_ref, kk_ref, ka_ref, vf_ref, v0_ref, v1_ref, v2_ref,
         r_o, lw_o, k_o, v_o, an_o, bb_o, g_o) = refs
    else:
        (h_ref, hp_ref, gn_ref, mu_ref, win_ref, w0_ref, w1_ref, w2_ref, a0_ref, a1_ref, a2_ref,
         g1_ref, g2_ref, kk_ref, ka_ref,
         r_o, lw_o, k_o, v_o, an_o, bb_o, g_o) = refs
    gn = gn_ref[...]
    xn = _rms(h_ref[...], gn)
    first = (pl.program_id(0) % tiles_per_seq) == 0
    prev = _rms(hp_ref[SHIFT_ROWS - 1:SHIFT_ROWS, :], gn)
    prev = jnp.where(first, 0.0, prev)
    xs = pltpu.roll(xn, 1, axis=0)
    xs = jnp.where(_iota(xn.shape, 0) == 0, prev, xs)
    dx = xs - xn
    xr, xw, xk, xv, xa, xg = (xn + dx * mu_ref[i:i + 1, :] for i in range(6))
    r = _dot(_bf(xr), win_ref[:, 0:D_MODEL])
    k = _dot(_bf(xk), win_ref[:, D_MODEL:2 * D_MODEL])
    v = _dot(_bf(xv), win_ref[:, 2 * D_MODEL:3 * D_MODEL])
    if vres:
        lo = _dot(_bf(_dot(_bf(xv), v1_ref[...])), v2_ref[...])
        v = v + (vf_ref[...] - v) * jax.nn.sigmoid(v0_ref[...] + lo)
    z = w0_ref[...] + _dot(_bf(jnp.tanh(_dot(_bf(xw), w1_ref[...]))), w2_ref[...])
    softplus = jnp.maximum(-z, 0.0) + jnp.log(1.0 + jnp.exp(-jnp.abs(z)))
    lw = -jnp.exp(-softplus - 0.5)
    a = jax.nn.sigmoid(a0_ref[...] + _dot(_bf(_dot(_bf(xa), a1_ref[...])), a2_ref[...]))
    g = _dot(_bf(jax.nn.sigmoid(_dot(_bf(xg), g1_ref[...]))), g2_ref[...])
    kk = k * kk_ref[...]
    nrm = jnp.sqrt(_seg64_sum(kk * kk, _bd64()))
    kk = kk / jnp.maximum(nrm, 1e-12)
    ii = _iota((RW_TM, RW_TM), 0)
    jj = _iota((RW_TM, RW_TM), 1)
    sh = WKV_CHUNK.bit_length() - 1
    cum_mat = jnp.where(((ii >> sh) == (jj >> sh)) & (jj <= ii), 1.0, 0.0).astype(BF16)
    hi = lw.astype(BF16)
    rem = lw - hi.astype(F32)
    mid = rem.astype(BF16)
    low = (rem - mid.astype(F32)).astype(BF16)
    r_o[...] = r
    lw_o[...] = _dot(cum_mat, hi) + _dot(cum_mat, mid) + _dot(cum_mat, low)
    k_o[...] = k * (1.0 + (a - 1.0) * ka_ref[...])
    v_o[...] = v
    an_o[...] = -kk
    bb_o[...] = kk * a
    g_o[...] = g


def _rwkv_prep(h, seq, gn, mu, win, w0, w1, w2, a0, a1, a2, g1, g2, k_k, k_a, vres):
    m = h.shape[0]
    row = lambda i: (i, 0)
    prev = lambda i: (jnp.maximum(i * (RW_TM // SHIFT_ROWS) - 1, 0), 0)
    vec = _const_spec((1, D_MODEL))
    full = lambda a: _const_spec(a.shape)
    args = [h, h, gn, mu, win, w0, w1, w2, a0, a1, a2, g1, g2, k_k, k_a]
    specs = [pl.BlockSpec((RW_TM, D_MODEL), row), pl.BlockSpec((SHIFT_ROWS, D_MODEL), prev), vec, full(mu),
             full(win), vec, full(w1), full(w2), vec, full(a1), full(a2), full(g1), full(g2), vec, vec]
    if vres is not None:
        v_first, v0, v1, v2 = vres
        args += [v_first, v0, v1, v2]
        specs += [pl.BlockSpec((RW_TM, D_MODEL), row), vec, full(v1), full(v2)]
    out = jax.ShapeDtypeStruct((m, D_MODEL), F32)
    return pl.pallas_call(
        functools.partial(_rwkv_prep_kernel, tiles_per_seq=seq // RW_TM, vres=vres is not None),
        grid=(m // RW_TM,),
        in_specs=specs,
        out_specs=[pl.BlockSpec((RW_TM, D_MODEL), row)] * 7,
        out_shape=[out] * 7,
        compiler_params=_params(("parallel",)),
    )(*args)


def _wkv_kernel(r_ref, cum_ref, k_ref, v_ref, a_ref, b_ref, g_ref, rk_ref, lnw_ref, lnb_ref,
                o_ref, s_ref, y_ref):
    L = WKV_CHUNK
    L2 = 2 * L

    @pl.when(pl.program_id(2) == 0)
    def _():
        s_ref[...] = jnp.zeros_like(s_ref)

    head0 = _iota((1, LANES), 1) < RWKV_N
    ii = _iota((L2, L2), 0)
    jj = _iota((L2, L2), 1)
    same = (ii >= L) == (jj >= L)
    tril_strict = same & (jj < ii)
    tril_incl = same & (jj <= ii)
    eye = jnp.where(ii == jj, 1.0, 0.0)
    first_row = _iota((L, LANES), 0) == 0
    bd = (_iota((LANES, LANES), 0) >> 6) == (_iota((LANES, LANES), 1) >> 6)

    def by_head(x):
        return jnp.concatenate([jnp.where(head0, x, 0.0), jnp.where(head0, 0.0, x)], axis=0)

    def twice(x):
        return jnp.concatenate([x, x], axis=0)

    def pick(x2):
        return jnp.where(head0, x2[:L], x2[L:])

    def chunk(sl, cols, s):
        cum = cum_ref[sl, cols]
        cum_ex = jnp.where(first_row, 0.0, pltpu.roll(cum, 1, axis=0))
        g_in = jnp.exp(cum)
        g_inv = jnp.exp(-cum)
        cum_l = cum[L - 1:L, :]
        g_to_end = jnp.exp(cum_l - cum)
        r = r_ref[sl, cols]
        k = k_ref[sl, cols]
        v = v_ref[sl, cols]
        b = b_ref[sl, cols]
        at = a_ref[sl, cols] * jnp.exp(cum_ex)
        rt = r * g_in
        bt = b * g_inv
        kt = k * g_inv
        sc = _dot_nt(_bf(jnp.concatenate([by_head(at), by_head(rt)], axis=0)),
                     _bf(jnp.concatenate([bt, bt, kt, kt], axis=0)))
        yield
        m_ab = jnp.where(tril_strict, sc[0:L2, 0:L2], 0.0)
        m_ak = jnp.where(tril_strict, sc[0:L2, L2:2 * L2], 0.0)
        n_rb = jnp.where(tril_incl, sc[L2:2 * L2, 0:L2], 0.0)
        n_rk = jnp.where(tril_incl, sc[L2:2 * L2, L2:2 * L2], 0.0)
        inv = eye + m_ab
        pw = m_ab
        for _ in range(int(np.log2(L)) - 1):
            pw = _dot(_bf(pw), _bf(pw))
            yield
            inv = inv + _dot(_bf(inv), _bf(pw))
            yield
        ps = _dot_nt(_bf(jnp.concatenate([at, rt], axis=0)), _bf(s))
        yield
        mv = _dot(_bf(jnp.concatenate([m_ak, n_rk], axis=0)), _bf(twice(v)))
        yield
        u = pick(_dot(_bf(inv), _bf(twice(ps[:L]) + mv[0:L2])))
        yield
        y_ref[sl, cols] = pick(twice(ps[L:]) + _dot(_bf(n_rb), _bf(twice(u))) + mv[L2:2 * L2])
        yield
        uv = jnp.concatenate([u, v], axis=0)
        bk = jnp.concatenate([b * g_to_end, k * g_to_end], axis=0)
        return s * jnp.exp(cum_l) + jnp.where(bd, _dot(_bf(uv.T), _bf(bk)), 0.0)

    ones = jnp.where(bd, 1.0, 0.0).astype(BF16)

    def pair(p):
        cols = slice(p * LANES, (p + 1) * LANES)
        s = s_ref[p]
        for c in range(WKV_TBLOCK // L):
            s = yield from chunk(pl.ds(c * L, L), cols, s)
            yield
        s_ref[p] = s
        y = y_ref[:, cols]
        mean = _dot_hl(y, ones) * (1.0 / RWKV_N)
        yield
        d = y - mean
        var = _dot_hl(d * d, ones) * (1.0 / RWKV_N)
        yield
        yn = d * lax.rsqrt(var + RWKV_GN_EPS) * lnw_ref[:, cols] + lnb_ref[:, cols]
        bonus = _dot_hl(r_ref[:, cols] * k_ref[:, cols] * rk_ref[:, cols], ones) * v_ref[:, cols]
        o_ref[:, cols] = _bf((yn + bonus) * g_ref[:, cols])

    running = [pair(p) for p in range(WKV_PAIRS)]
    done = object()
    while running:
        running = [gen for gen in running if next(gen, done) is not done]


def _wkv(batch, seq, r, cum, k, v, an, bb, g, r_k, ln_w, ln_b):
    m = r.shape[0]
    nt = seq // WKV_TBLOCK
    width = WKV_PAIRS * LANES
    blk = pl.BlockSpec((WKV_TBLOCK, width), lambda b, p, i: (b * nt + i, p))
    vec = pl.BlockSpec((1, width), lambda b, p, i: (0, p))
    return pl.pallas_call(
        _wkv_kernel,
        grid=(batch, D_MODEL // width, nt),
        in_specs=[blk] * 7 + [vec] * 3,
        out_specs=blk,
        out_shape=jax.ShapeDtypeStruct((m, D_MODEL), BF16),
        scratch_shapes=[pltpu.VMEM((WKV_PAIRS, LANES, LANES), F32), pltpu.VMEM((WKV_TBLOCK, width), F32)],
        compiler_params=_params(("parallel", "parallel", "arbitrary")),
    )(r, cum, k, v, an, bb, g, r_k, ln_w, ln_b)


ATT_TQ = 256
ATT_TK = 256
NSA_TM = ATT_TK
N_POS_LANES = 3
KVA_HEAD_COLS = 4 * LANES
KVA_COLS = NSA_HKV * KVA_HEAD_COLS
KVC_COLS = NSA_HKV * LANES
QA_COLS = NSA_HKV * NSA_G * LANES
GATE_COLS = NSA_HKV * LANES
COL_PLAIN, COL_KEY_NORM, COL_POS_MOD, COL_POS_TILE, COL_ONE = range(5)


def _nsa_proj_kernel(h_ref, gn_ref, wq_ref, wkva_ref, wkvc_ref, wg_ref, qg_ref, qc_ref, kg_ref, km_ref,
                     q_o, kva_o, kvc_o, gt_o, *, tiles_per_seq):
    xb = _bf(_rms(h_ref[...], gn_ref[...]))
    bd = _bd64()
    inv_dh = 1.0 / NSA_DH
    q = _dot(xb, wq_ref[...])
    q_o[...] = _bf(q * lax.rsqrt(_seg64_sum(q * q, bd) * inv_dh + NORM_EPS) * qg_ref[...] + qc_ref[...])
    kv = _dot(xb, wkva_ref[...])
    kvn = kv * lax.rsqrt(_seg64_sum(kv * kv, bd) * inv_dh + NORM_EPS) * kg_ref[...]
    kind = km_ref[...]
    pos_mod = _iota(kv.shape, 0).astype(F32)
    pos_tile = (pl.program_id(0) % tiles_per_seq).astype(F32)
    kv = jnp.where(kind == COL_KEY_NORM, kvn, kv)
    kv = jnp.where(kind == COL_POS_MOD, pos_mod, kv)
    kv = jnp.where(kind == COL_POS_TILE, pos_tile, kv)
    kva_o[...] = _bf(jnp.where(kind == COL_ONE, 1.0, kv))
    kvc_o[...] = _dot(xb, wkvc_ref[...])
    gt_o[...] = jax.nn.sigmoid(_dot(xb, wg_ref[...]))


def _nsa_proj(h, seq, gn, wq, wkva, wkvc, wg, qg, qc, kg, km):
    m = h.shape[0]
    row = lambda i: (i, 0)
    return pl.pallas_call(
        functools.partial(_nsa_proj_kernel, tiles_per_seq=seq // NSA_TM),
        grid=(m // NSA_TM,),
        in_specs=[pl.BlockSpec((NSA_TM, D_MODEL), row), _const_spec((1, D_MODEL)),
                  _const_spec(wq.shape), _const_spec(wkva.shape), _const_spec(wkvc.shape), _const_spec(wg.shape),
                  _const_spec((1, QA_COLS)), _const_spec((1, QA_COLS)),
                  _const_spec((1, KVA_COLS)), _const_spec((1, KVA_COLS))],
        out_specs=[pl.BlockSpec((NSA_TM, QA_COLS), row), pl.BlockSpec((NSA_TM, KVA_COLS), row),
                   pl.BlockSpec((NSA_TM, KVC_COLS), row), pl.BlockSpec((NSA_TM, GATE_COLS), row)],
        out_shape=[jax.ShapeDtypeStruct((m, QA_COLS), BF16), jax.ShapeDtypeStruct((m, KVA_COLS), BF16),
                   jax.ShapeDtypeStruct((m, KVC_COLS), F32), jax.ShapeDtypeStruct((m, GATE_COLS), F32)],
        compiler_params=_params(("parallel",)),
    )(h, gn, wq, wkva, wkvc, wg, qg, qc, kg, km)


def _nsa_cmp_kernel(kv_ref, wa_ref, wb_ref, pa_ref, pb_ref, w2_ref, kg_ref, o_ref, *, n_blk):
    nb = n_blk + 1
    first = jnp.zeros((nb, 2 * CMP_HID), F32)
    second = jnp.zeros((nb, 2 * CMP_HID), F32)
    for p in range(CMP_STRIDE):
        x = kv_ref[pl.ds(p, nb, stride=CMP_STRIDE), :]
        first = first + _dot(_bf(x + pa_ref[p:p + 1, :]), wa_ref[p])
        second = second + _dot(_bf(x + pb_ref[p:p + 1, :]), wb_ref[p])
    hid = jax.nn.gelu(first + pltpu.roll(second, nb - 1, axis=0))
    out = _dot(_bf(hid), w2_ref[...])
    ss = _seg64_sum(out * out, _bd64()) * (1.0 / NSA_DH)
    lane = _iota((nb, 2 * LANES), 1)
    out = jnp.where(lane < NSA_DH, out * lax.rsqrt(ss + NORM_EPS) * kg_ref[...], out)
    end = _iota((nb, 2 * LANES), 0) * CMP_STRIDE + (2 * CMP_STRIDE - 1)
    sh = ATT_TK.bit_length() - 1
    out = jnp.where((lane >= NSA_DH) & (lane < NSA_DH + N_POS_LANES), (end & (ATT_TK - 1)).astype(F32), out)
    out = jnp.where((lane >= NSA_DH + N_POS_LANES) & (lane < NSA_DH + 2 * N_POS_LANES), (end >> sh).astype(F32), out)
    o_ref[...] = _bf(out)


def _nsa_cmp(batch, seq, kv, wa, wb, pa, pb, w2, kg):
    n_blk = seq // CMP_STRIDE - 1
    nb = n_blk + 1
    return pl.pallas_call(
        functools.partial(_nsa_cmp_kernel, n_blk=n_blk),
        grid=(batch, NSA_HKV),
        in_specs=[pl.BlockSpec((seq, LANES), lambda b, h: (b, h)),
                  _const_spec(wa.shape), _const_spec(wb.shape), _const_spec(pa.shape), _const_spec(pb.shape),
                  _const_spec(w2.shape), _const_spec(kg.shape)],
        out_specs=pl.BlockSpec((None, None, nb, 2 * LANES), lambda b, h: (b, h, 0, 0)),
        out_shape=jax.ShapeDtypeStruct((batch, NSA_HKV, nb, 2 * LANES), BF16),
        compiler_params=_params(("parallel", "parallel")),
    )(kv, wa, wb, pa, pb, w2, kg)


def _nsa_attn_kernel(q_ref, kv_ref, oh_ref, cmp_ref, gt_ref, o_ref,
                     qam_s, m_s, acc_s, sa_s, sb_s, ow_s, tiles_s, *, n_blk, n_sel, n_slc):
    tq, tk = ATT_TQ, ATT_TK
    R = NSA_G * tq
    qi = pl.program_id(2)
    q0 = qi * tq
    t_one = q0 + _iota((tq, 1), 0)
    gt = gt_ref[...]
    n_ch = NSA_G
    qa = jnp.concatenate([q_ref[:, g * LANES:(g + 1) * LANES] for g in range(NSA_G)], axis=0)
    qam_s[:, 0:LANES] = qa

    def window():
        n_win = WINDOW // tk + 1
        w0 = pl.multiple_of(jnp.maximum(qi - (n_win - 1), 0) * tk, tk)
        wrows = pl.ds(w0, n_win * tk)
        dist = t_one - (w0 + _iota((1, n_win * tk), 1))
        in_window = dist.astype(jnp.uint32) < WINDOW
        pes = []
        s_all = _dot_nt(qa, kv_ref[wrows, LANES:2 * LANES])
        yield
        for c in range(n_ch):
            s = jnp.where(in_window, s_all[c * tq:(c + 1) * tq], MASK_NEG)
            pes.append(_bf(jnp.exp2(s - jnp.max(s, axis=-1, keepdims=True))))
            yield
        pv = _dot(jnp.concatenate(pes, axis=0), kv_ref[wrows, 3 * LANES:4 * LANES]).reshape(NSA_G, tq, LANES)
        yield
        for g in range(NSA_G):
            ow_s[g] = (gt[:, 3 * g + 2:3 * g + 3] / pv[g][:, NSA_DH:NSA_DH + 1]) * pv[g][:, 0:NSA_DH]

    picked = {}

    def compressed():
        nb = cmp_ref.shape[0]
        n_idx = _iota((1, nb), 1)
        ps = []
        valid = (n_idx * CMP_STRIDE + (2 * CMP_STRIDE - 1) <= t_one) & (n_idx < n_blk)
        qk = _dot_nt(qa, cmp_ref[:, 0:LANES])
        yield
        for g in range(NSA_G):
            s = jnp.where(valid, qk[g * tq:(g + 1) * tq], -jnp.inf)
            mx = jnp.max(s, axis=-1, keepdims=True)
            mx = jnp.where(mx == -jnp.inf, 0.0, mx)
            e = jnp.where(valid, jnp.exp2(s - mx), 0.0)
            ps.append(e / jnp.maximum(jnp.sum(e, axis=-1, keepdims=True), 1e-30))
            yield
        picked["o_cmp"] = _dot(_bf(jnp.concatenate(ps, axis=0)), cmp_ref[:, LANES:LANES + NSA_DH])
        yield
        psum = ps[0] + ps[1] + ps[2] + ps[3]
        sj = _iota((n_slc, nb), 0) * SLC_BLOCK
        cn = _iota((n_slc, nb), 1) * CMP_STRIDE
        overlap_t = jnp.where((cn <= sj + SLC_BLOCK - 1) & (cn + 2 * CMP_STRIDE - 1 >= sj), 1.0, 0.0).astype(BF16)
        p_hi, p_lo = _split2(psum)
        imp = _dot_nt(overlap_t, p_hi) + _dot_nt(overlap_t, p_lo)
        yield
        blk = _iota((n_slc, 1), 0)
        ahead_of_cur = ((q0 + _iota((1, tq), 1)) >> 6) - blk
        forced = (blk * ahead_of_cur * (ahead_of_cur - 1)) == 0
        imp = jnp.where(forced, SEL_BIG, imp)
        imp = jnp.where(ahead_of_cur >= 0, imp, -SEL_BIG)
        sub = 8
        ranks = []
        for lo in range(0, n_slc, sub):
            mine = imp[lo:lo + sub, :]
            rk = jnp.zeros((sub, tq), F32)
            for j in range(n_slc):
                other = imp[j:j + 1, :]
                if j < lo:
                    rk = rk + jnp.where(other >= mine, 1.0, 0.0)
                elif j >= lo + sub:
                    rk = rk + jnp.where(other > mine, 1.0, 0.0)
                else:
                    rk = rk + jnp.where(blk[lo:lo + sub] > j, jnp.where(other >= mine, 1.0, 0.0),
                                        jnp.where(other > mine, 1.0, 0.0))
            ranks.append(rk)
            yield
        rank = jnp.concatenate(ranks, axis=0)
        picked["not_sel_t"] = jnp.where(rank < n_sel, 0.0, 1.0)

    running = [window(), compressed()]
    done = object()
    while running:
        running = [gen for gen in running if next(gen, done) is not done]
    o_cmp, not_sel_t = picked["o_cmp"], picked["not_sel_t"]
    qm = jnp.concatenate([not_sel_t.T, jnp.zeros((tq, LANES - n_slc), F32)], axis=1)
    qm = _bf(jnp.where(_iota((tq, LANES), 1) == PAD_TILE_LANE, 1.0, qm))
    for g in range(NSA_G):
        qam_s[g * tq:(g + 1) * tq, LANES:2 * LANES] = qm

    n_kt = n_slc * SLC_BLOCK // tk
    in_tile = (_iota((n_kt, n_slc), 1) * SLC_BLOCK // tk) == _iota((n_kt, n_slc), 0)
    picks = _dot(jnp.where(in_tile, 1.0, 0.0).astype(BF16), _bf(1.0 - not_sel_t))
    needed = jnp.max(picks, axis=1, keepdims=True)
    n_int = jnp.int32(0)
    for kt in range(n_kt):
        tiles_s[n_int] = kt
        n_int = n_int + ((needed[kt, 0] > 0.0) & (kt < qi)).astype(jnp.int32)
    n_even = n_int + (n_int & 1)
    tiles_s[n_int] = n_kt
    tiles_s[n_even] = qi
    tiles_s[n_even + 1] = qi

    m_s[...] = jnp.full_like(m_s, MASK_NEG)
    acc_s[...] = jnp.zeros_like(acc_s)

    def key_rows(j):
        kt = tiles_s[j]
        return (pl.ds(pl.multiple_of(jnp.minimum(kt, n_kt - 1) * tk, tk), tk),
                pl.ds(pl.multiple_of(kt * tk, tk), tk))

    def scores(j, s_ref):
        kv_rows, oh_rows = key_rows(j)
        kaug = jnp.concatenate([kv_ref[kv_rows, 0:LANES], oh_ref[oh_rows, :]], axis=1)
        s_ref[...] = _dot_nt(qam_s[...], kaug)

    def softmax_pv(j, s_ref, diagonal):
        kv_rows, _ = key_rows(j)
        pes, alphas = [], []
        causal = q0 + _iota((1, tk), 1) <= t_one
        for c in range(n_ch):
            s = s_ref[c * tq:(c + 1) * tq, :]
            if diagonal:
                s = jnp.where(causal, s, MASK_NEG)
            m_old = m_s[c]
            m_new = jnp.maximum(m_old, jnp.max(s, axis=-1, keepdims=True))
            pes.append(_bf(jnp.exp2(s - jnp.concatenate([m_new] * (tk // LANES), axis=1))))
            alphas.append(jnp.exp2(m_old - m_new))
            m_s[c] = m_new
        pv = _dot(jnp.concatenate(pes, axis=0), kv_ref[kv_rows, 2 * LANES:3 * LANES])
        for c in range(n_ch):
            acc_s[c] = alphas[c] * acc_s[c] + pv[c * tq:(c + 1) * tq]

    def sel_body(jj, carry):
        j = 2 * jj
        scores(j + 1, sb_s)
        softmax_pv(j, sa_s, False)
        scores(j + 2, sa_s)
        softmax_pv(j + 1, sb_s, False)
        return carry

    scores(0, sa_s)
    lax.fori_loop(0, n_even // 2, sel_body, 0)
    softmax_pv(n_even, sa_s, True)
    acc = acc_s[...].reshape(NSA_G, tq, LANES)

    outs = []
    for g in range(NSA_G):
        w_sel = gt[:, 3 * g + 1:3 * g + 2] / acc[g][:, NSA_DH:NSA_DH + 1]
        outs.append(gt[:, 3 * g:3 * g + 1] * o_cmp[g * tq:(g + 1) * tq] + w_sel * acc[g][:, 0:NSA_DH]
                    + ow_s[g])
    o_ref[...] = _bf(jnp.concatenate(outs, axis=1))


def _nsa_attn(batch, seq, q, kva, onehot, cmp, gt):
    assert ATT_TQ == ATT_TK
    m = q.shape[0]
    nq = seq // ATT_TQ
    n_blk = seq // CMP_STRIDE - 1
    n_slc = seq // SLC_BLOCK
    n_sel = min(N_SELECT, n_slc)
    R = NSA_G * ATT_TQ
    tile = lambda b, h, i: (b * nq + i, h)
    return pl.pallas_call(
        functools.partial(_nsa_attn_kernel, n_blk=n_blk, n_sel=n_sel, n_slc=n_slc),
        grid=(batch, NSA_HKV, nq),
        in_specs=[pl.BlockSpec((ATT_TQ, NSA_G * LANES), tile),
                  pl.BlockSpec((seq, KVA_HEAD_COLS), lambda b, h, i: (b, h)),
                  pl.BlockSpec((seq + ATT_TK, LANES), lambda b, h, i: (0, 0)),
                  pl.BlockSpec((None, None, n_blk + 1, 2 * LANES), lambda b, h, i: (b, h, 0, 0)),
                  pl.BlockSpec((ATT_TQ, LANES), tile)],
        out_specs=pl.BlockSpec((ATT_TQ, NSA_G * NSA_DH), tile),
        out_shape=jax.ShapeDtypeStruct((m, D_MODEL), BF16),
        scratch_shapes=[pltpu.VMEM((R, 2 * LANES), BF16),
                        pltpu.VMEM((NSA_G, ATT_TQ, LANES), F32),
                        pltpu.VMEM((NSA_G, ATT_TQ, LANES), F32),
                        pltpu.VMEM((R, ATT_TK), F32), pltpu.VMEM((R, ATT_TK), F32),
                        pltpu.VMEM((NSA_G, ATT_TQ, NSA_DH), F32),
                        pltpu.SMEM((seq // ATT_TK + 3,), jnp.int32)],
        compiler_params=_params(("parallel", "parallel", "arbitrary")),
    )(q, kva, onehot, cmp, gt)


def _pad_cols(w, n):
    return jnp.pad(w, ((0, 0), (0, n - w.shape[1])))


def _pad_rows(w, n):
    return jnp.pad(w, ((0, n - w.shape[0]), (0, 0)))


def _lora(w_down, w_up):
    rank = -(-w_down.shape[1] // LANES) * LANES
    return _bf(_pad_cols(w_down, rank)), _bf(_pad_rows(w_up, rank))


def _rwkv_layer(h, batch, seq, gn, mu, w_in, w0, w1, w2, a0, a1, a2, g1, g2, k_k, k_a, r_k, ln_w, ln_b,
                v_first, vres):
    vec = lambda x: x.reshape(1, D_MODEL)
    w1p, w2p = _lora(w1, w2)
    a1p, a2p = _lora(a1, a2)
    g1p, g2p = _lora(g1, g2)
    if vres is not None:
        v0, v1, v2 = vres
        v1p, v2p = _lora(v1, v2)
        vres = (v_first, vec(v0), v1p, v2p)
    mu8 = jnp.pad(mu, ((0, 2), (0, 0)))
    r, lw, k, v, an, bb, g = _rwkv_prep(h, seq, vec(gn), mu8, _bf(w_in), vec(w0), w1p, w2p, vec(a0), a1p, a2p,
                                        g1p, g2p, vec(k_k), vec(k_a), vres)
    y = _wkv(batch, seq, r, lw, k, v, an, bb, g, vec(r_k), vec(ln_w), vec(ln_b))
    return y, v


def _nsa_layer(h, batch, seq, gn, w_in, cmp_pe, cmp_w1, cmp_w2, q_norm, k_norm):
    hkv, dh = NSA_HKV, NSA_DH
    nq = D_MODEL
    n_heads = hkv * NSA_G
    kv_cols = hkv * dh
    one = jnp.ones((dh,), F32)
    zero = jnp.zeros((dh,), F32)
    lane = jnp.arange(dh)
    wq = jnp.pad(w_in[:, :nq].reshape(D_MODEL, n_heads, dh), ((0, 0), (0, 0), (0, LANES - dh))).reshape(D_MODEL, QA_COLS)
    qg = jnp.tile(jnp.concatenate([q_norm * Q_PRESCALE, zero]), n_heads).reshape(1, QA_COLS)
    slopes = jnp.exp2(-8.0 * (jnp.arange(n_heads, dtype=F32) + 1.0) / n_heads) * LOG2_E
    s1 = slopes.astype(BF16).astype(F32)
    s2 = (slopes - s1).astype(BF16).astype(F32)
    s3 = (slopes - s1 - s2).astype(BF16).astype(F32)
    terms = jnp.stack([s1, s2, s3], axis=1)
    qc = jnp.concatenate([jnp.zeros((n_heads, dh), F32), terms, terms * ATT_TK,
                          jnp.zeros((n_heads, LANES - dh - 2 * N_POS_LANES), F32)], axis=1).reshape(1, QA_COLS)

    wkv = w_in[:, nq:nq + 6 * kv_cols].reshape(D_MODEL, 6, hkv, dh)
    zpad = jnp.zeros((D_MODEL, hkv, dh), F32)
    wkva = jnp.stack([wkv[:, 2], zpad, wkv[:, 4], zpad, wkv[:, 3], zpad, wkv[:, 5], zpad],
                     axis=2).reshape(D_MODEL, KVA_COLS)
    wkvc = jnp.stack([wkv[:, 0], wkv[:, 1]], axis=2).reshape(D_MODEL, KVC_COLS)
    wg = w_in[:, nq + 6 * kv_cols:].reshape(D_MODEL, hkv, 3 * NSA_G)
    wg = jnp.pad(wg, ((0, 0), (0, 0), (0, LANES - 3 * NSA_G))).reshape(D_MODEL, GATE_COLS)
    pos_kind = jnp.where(lane < N_POS_LANES, COL_POS_MOD, jnp.where(lane < 2 * N_POS_LANES, COL_POS_TILE, COL_PLAIN))
    one_kind = jnp.where(lane == 0, COL_ONE, COL_PLAIN)
    key_kind = jnp.full((dh,), COL_KEY_NORM)
    plain = jnp.full((dh,), COL_PLAIN)
    kg = jnp.tile(jnp.concatenate([k_norm[1], one, k_norm[2], one, one, one, one, one]), hkv).reshape(1, KVA_COLS)
    km = jnp.tile(jnp.concatenate([key_kind, pos_kind, key_kind, pos_kind, plain, one_kind, plain, one_kind]),
                  hkv).reshape(1, KVA_COLS).astype(jnp.int32)
    q, kva, kvc, gt = _nsa_proj(h, seq, gn.reshape(1, D_MODEL), _bf(wq), _bf(wkva), _bf(wkvc), _bf(wg),
                                qg, qc, kg, km)

    w1 = cmp_w1.reshape(2, 2 * CMP_STRIDE, dh, CMP_HID)
    zeros = jnp.zeros((2 * CMP_STRIDE, dh, CMP_HID), F32)
    wfull = jnp.concatenate([jnp.concatenate([w1[0], zeros], axis=2),
                             jnp.concatenate([zeros, w1[1]], axis=2)], axis=1)
    pe = jnp.concatenate([cmp_pe[0], cmp_pe[1]], axis=1)
    w2 = jnp.zeros((2 * CMP_HID, 2 * LANES), F32)
    w2 = w2.at[:CMP_HID, :dh].set(cmp_w2[0]).at[CMP_HID:, LANES:LANES + dh].set(cmp_w2[1])
    kg0 = jnp.concatenate([k_norm[0], one, one, one]).reshape(1, 2 * LANES)
    cmp = _nsa_cmp(batch, seq, kvc, _bf(wfull[:CMP_STRIDE]), _bf(wfull[CMP_STRIDE:]),
                   pe[:CMP_STRIDE], pe[CMP_STRIDE:], _bf(w2), kg0)

    key = jnp.arange(seq + ATT_TK)[:, None]
    lanes = jnp.arange(LANES)[None, :]
    onehot = jnp.where(key < seq, lanes == key // SLC_BLOCK, lanes == PAD_TILE_LANE)
    return _nsa_attn(batch, seq, q, kva, _bf(jnp.where(onehot, MASK_NEG, 0.0)), cmp, gt)


def kernel(x, mix_norm, mlp_norm, mlp_w1, mlp_w2, rwkv_mu, rwkv_w_in, rwkv_w0, rwkv_w1, rwkv_w2, rwkv_a0, rwkv_a1, rwkv_a2, rwkv_v0, rwkv_v1, rwkv_v2, rwkv_g1, rwkv_g2, rwkv_k_k, rwkv_k_a, rwkv_r_k, rwkv_ln_w, rwkv_ln_b, rwkv_w_out, nsa_w_in, nsa_cmp_pe, nsa_cmp_w1, nsa_cmp_w2, nsa_q_norm, nsa_k_norm, nsa_w_out):
    batch, seq, d = x.shape
    depth = mix_norm.shape[0]
    h = x.reshape(batch * seq, d)
    v_first = None
    for i in range(depth):
        j = i // 2
        if i % 2 == 0:
            vres = None if j == 0 else (rwkv_v0[j - 1], rwkv_v1[j - 1], rwkv_v2[j - 1])
            a, v_raw = _rwkv_layer(h, batch, seq, mix_norm[i], rwkv_mu[j], rwkv_w_in[j], rwkv_w0[j], rwkv_w1[j],
                                   rwkv_w2[j], rwkv_a0[j], rwkv_a1[j], rwkv_a2[j], rwkv_g1[j], rwkv_g2[j],
                                   rwkv_k_k[j], rwkv_k_a[j], rwkv_r_k[j].reshape(-1), rwkv_ln_w[j], rwkv_ln_b[j],
                                   v_first, vres)
            if j == 0:
                v_first = v_raw
            wo = rwkv_w_out[j]
        else:
            a = _nsa_layer(h, batch, seq, mix_norm[i], nsa_w_in[j], nsa_cmp_pe[j], nsa_cmp_w1[j], nsa_cmp_w2[j],
                           nsa_q_norm[j], nsa_k_norm[j])
            wo = nsa_w_out[j]
        h = _out_mlp(h, a, _bf(wo), mlp_norm[i].reshape(1, d), _bf(mlp_w1[i]), _bf(mlp_w2[i]))
    return h.reshape(batch, seq, d)
```

```python
import functools

import jax
import jax.numpy as jnp
import numpy as np
from jax import lax
from jax.experimental import pallas as pl
from jax.experimental.pallas import tpu as pltpu

F32 = jnp.float32
BF16 = jnp.bfloat16

D_MODEL = 1024
MLP_HIDDEN = 4 * D_MODEL
NORM_EPS = 1e-6
LANES = 128
VMEM_LIMIT = 56 * 1024 * 1024

RWKV_N = 64
RWKV_GN_EPS = 64e-5
WKV_CHUNK = 64
WKV_TBLOCK = 128
WKV_PAIRS = 8

NSA_HKV = 4
NSA_G = 4
NSA_DH = 64
CMP_STRIDE = 16
CMP_HID = 128
SLC_BLOCK = 64
N_SELECT = 16
WINDOW = 512
SEL_BIG = 1e9
MASK_NEG = -1e30
LOG2_E = 1.4426950408889634
Q_PRESCALE = NSA_DH ** -0.5 * LOG2_E
PAD_TILE_LANE = LANES - 1


def _dot(a, b):
    return jnp.dot(a, b, preferred_element_type=F32)


def _dot_nt(a, b):
    return lax.dot_general(a, b, (((1,), (1,)), ((), ())), preferred_element_type=F32)


def _bf(x):
    return x.astype(BF16)


def _split2(x):
    hi = x.astype(BF16)
    lo = (x - hi.astype(F32)).astype(BF16)
    return hi, lo


def _dot_hl(x, w):
    hi, lo = _split2(x)
    return _dot(hi, w) + _dot(lo, w)


def _dot_hl_rev(w, x):
    hi, lo = _split2(x)
    return _dot(w, hi) + _dot(w, lo)


def _iota(shape, dim):
    return lax.broadcasted_iota(jnp.int32, shape, dim)


def _bd64(n=LANES):
    return jnp.where((_iota((n, n), 0) >> 6) == (_iota((n, n), 1) >> 6), 1.0, 0.0).astype(BF16)


def _seg64_sum(x, bd, keep=lambda c: True):
    outs = [_dot(_bf(x[:, c * LANES:(c + 1) * LANES]), bd) if keep(c) else
            jnp.zeros((x.shape[0], LANES), F32) for c in range(x.shape[1] // LANES)]
    return outs[0] if len(outs) == 1 else jnp.concatenate(outs, axis=1)


def _rms(x, g):
    return x * lax.rsqrt(jnp.mean(x * x, axis=-1, keepdims=True) + NORM_EPS) * g


def _const_spec(shape):
    return pl.BlockSpec(shape, lambda *_: (0,) * len(shape))


def _params(sem):
    return pltpu.CompilerParams(dimension_semantics=sem, vmem_limit_bytes=VMEM_LIMIT)


MLP_TM = 512
MLP_TH = 1024


def _out_mlp_kernel(h_ref, a_ref, wo_ref, g_ref, w1_ref, w2_ref, o_ref):
    h1 = h_ref[...] + _dot(a_ref[...], wo_ref[...])
    xb = _bf(_rms(h1, g_ref[...]))
    acc = h1
    for c in range(MLP_HIDDEN // MLP_TH):
        u = jnp.maximum(_dot(xb, w1_ref[:, c * MLP_TH:(c + 1) * MLP_TH]), 0.0)
        acc = acc + _dot(_bf(u * u), w2_ref[c * MLP_TH:(c + 1) * MLP_TH, :])
    o_ref[...] = acc


def _out_mlp(h, a, wo, g, w1, w2):
    m = h.shape[0]
    row = lambda i: (i, 0)
    return pl.pallas_call(
        _out_mlp_kernel,
        grid=(m // MLP_TM,),
        in_specs=[pl.BlockSpec((MLP_TM, D_MODEL), row), pl.BlockSpec((MLP_TM, D_MODEL), row),
                  _const_spec((D_MODEL, D_MODEL)), _const_spec((1, D_MODEL)),
                  _const_spec((D_MODEL, MLP_HIDDEN)), _const_spec((MLP_HIDDEN, D_MODEL))],
        out_specs=pl.BlockSpec((MLP_TM, D_MODEL), row),
        out_shape=jax.ShapeDtypeStruct((m, D_MODEL), F32),
        compiler_params=_params(("parallel",)),
    )(h, a, wo, g, w1, w2)


RW_TM = 256
SHIFT_ROWS = 8


def _rwkv_prep_kernel(*refs, tiles_per_seq, vres):
    if vres:
        (h_ref, hp_ref, gn_ref, mu_ref, win_ref, w0_ref, w1_ref, w2_ref, a0_ref, a1_ref, a2_ref,
         g1_ref, g2_ref, kk_ref, ka_ref, vf_ref, v0_ref, v1_ref, v2_ref,
         r_o, lw_o, k_o, v_o, an_o, bb_o, g_o) = refs
    else:
        (h_ref, hp_ref, gn_ref, mu_ref, win_ref, w0_ref, w1_ref, w2_ref, a0_ref, a1_ref, a2_ref,
         g1_ref, g2_ref, kk_ref, ka_ref,
         r_o, lw_o, k_o, v_o, an_o, bb_o, g_o) = refs
    gn = gn_ref[...]
    xn = _rms(h_ref[...], gn)
    first = (pl.program_id(0) % tiles_per_seq) == 0
    prev = _rms(hp_ref[SHIFT_ROWS - 1:SHIFT_ROWS, :], gn)
    prev = jnp.where(first, 0.0, prev)
    xs = pltpu.roll(xn, 1, axis=0)
    xs = jnp.where(_iota(xn.shape, 0) == 0, prev, xs)
    dx = xs - xn
    xr, xw, xk, xv, xa, xg = (xn + dx * mu_ref[i:i + 1, :] for i in range(6))
    r = _dot(_bf(xr), win_ref[:, 0:D_MODEL])
    k = _dot(_bf(xk), win_ref[:, D_MODEL:2 * D_MODEL])
    v = _dot(_bf(xv), win_ref[:, 2 * D_MODEL:3 * D_MODEL])
    if vres:
        lo = _dot(_bf(_dot(_bf(xv), v1_ref[...])), v2_ref[...])
        v = v + (vf_ref[...] - v) * jax.nn.sigmoid(v0_ref[...] + lo)
    z = w0_ref[...] + _dot(_bf(jnp.tanh(_dot(_bf(xw), w1_ref[...]))), w2_ref[...])
    softplus = jnp.maximum(-z, 0.0) + jnp.log(1.0 + jnp.exp(-jnp.abs(z)))
    lw = -jnp.exp(-softplus - 0.5)
    a = jax.nn.sigmoid(a0_ref[...] + _dot(_bf(_dot(_bf(xa), a1_ref[...])), a2_ref[...]))
    g = _dot(_bf(jax.nn.sigmoid(_dot(_bf(xg), g1_ref[...]))), g2_ref[...])
    kk = k * kk_ref[...]
    nrm = jnp.sqrt(_seg64_sum(kk * kk, _bd64()))
    kk = kk / jnp.maximum(nrm, 1e-12)
    ii = _iota((RW_TM, RW_TM), 0)
    jj = _iota((RW_TM, RW_TM), 1)
    sh = WKV_CHUNK.bit_length() - 1
    cum_mat = jnp.where(((ii >> sh) == (jj >> sh)) & (jj <= ii), 1.0, 0.0).astype(BF16)
    r_o[...] = r
    lw_o[...] = _dot_hl_rev(cum_mat, lw)
    k_o[...] = k * (1.0 + (a - 1.0) * ka_ref[...])
    v_o[...] = v
    an_o[...] = -kk
    bb_o[...] = kk * a
    g_o[...] = g


def _rwkv_prep(h, seq, gn, mu, win, w0, w1, w2, a0, a1, a2, g1, g2, k_k, k_a, vres):
    m = h.shape[0]
    row = lambda i: (i, 0)
    prev = lambda i: (jnp.maximum(i * (RW_TM // SHIFT_ROWS) - 1, 0), 0)
    vec = _const_spec((1, D_MODEL))
    full = lambda a: _const_spec(a.shape)
    args = [h, h, gn, mu, win, w0, w1, w2, a0, a1, a2, g1, g2, k_k, k_a]
    specs = [pl.BlockSpec((RW_TM, D_MODEL), row), pl.BlockSpec((SHIFT_ROWS, D_MODEL), prev), vec, full(mu),
             full(win), vec, full(w1), full(w2), vec, full(a1), full(a2), full(g1), full(g2), vec, vec]
    if vres is not None:
        v_first, v0, v1, v2 = vres
        args += [v_first, v0, v1, v2]
        specs += [pl.BlockSpec((RW_TM, D_MODEL), row), vec, full(v1), full(v2)]
    out = jax.ShapeDtypeStruct((m, D_MODEL), F32)
    return pl.pallas_call(
        functools.partial(_rwkv_prep_kernel, tiles_per_seq=seq // RW_TM, vres=vres is not None),
        grid=(m // RW_TM,),
        in_specs=specs,
        out_specs=[pl.BlockSpec((RW_TM, D_MODEL), row)] * 7,
        out_shape=[out] * 7,
        compiler_params=_params(("parallel",)),
    )(*args)


def _wkv_kernel(r_ref, cum_ref, k_ref, v_ref, a_ref, b_ref, g_ref, rk_ref, lnw_ref, lnb_ref,
                o_ref, s_ref, y_ref):
    L = WKV_CHUNK
    L2 = 2 * L

    @pl.when(pl.program_id(2) == 0)
    def _():
        s_ref[...] = jnp.zeros_like(s_ref)

    head0 = _iota((1, LANES), 1) < RWKV_N
    ii = _iota((L2, L2), 0)
    jj = _iota((L2, L2), 1)
    same = (ii >= L) == (jj >= L)
    tril_strict = same & (jj < ii)
    tril_incl = same & (jj <= ii)
    eye = jnp.where(ii == jj, 1.0, 0.0)
    first_row = _iota((L, LANES), 0) == 0
    bd = (_iota((LANES, LANES), 0) >> 6) == (_iota((LANES, LANES), 1) >> 6)

    def by_head(x):
        return jnp.concatenate([jnp.where(head0, x, 0.0), jnp.where(head0, 0.0, x)], axis=0)

    def twice(x):
        return jnp.concatenate([x, x], axis=0)

    def pick(x2):
        return jnp.where(head0, x2[:L], x2[L:])

    def chunk(sl, cols, s):
        cum = cum_ref[sl, cols]
        cum_ex = jnp.where(first_row, 0.0, pltpu.roll(cum, 1, axis=0))
        g_in = jnp.exp(cum)
        g_inv = jnp.exp(-cum)
        cum_l = cum[L - 1:L, :]
        g_to_end = jnp.exp(cum_l - cum)
        r = r_ref[sl, cols]
        k = k_ref[sl, cols]
        v = v_ref[sl, cols]
        b = b_ref[sl, cols]
        at = a_ref[sl, cols] * jnp.exp(cum_ex)
        rt = r * g_in
        bt = b * g_inv
        kt = k * g_inv
        sc = _dot_nt(_bf(jnp.concatenate([by_head(at), by_head(rt)], axis=0)),
                     _bf(jnp.concatenate([bt, bt, kt, kt], axis=0)))
        yield
        m_ab = jnp.where(tril_strict, sc[0:L2, 0:L2], 0.0)
        m_ak = jnp.where(tril_strict, sc[0:L2, L2:2 * L2], 0.0)
        n_rb = jnp.where(tril_incl, sc[L2:2 * L2, 0:L2], 0.0)
        n_rk = jnp.where(tril_incl, sc[L2:2 * L2, L2:2 * L2], 0.0)
        inv = eye + m_ab
        pw = _dot(_bf(m_ab), _bf(m_ab))
        yield
        for _ in range(int(np.log2(L)) - 2):
            both = _dot(_bf(jnp.concatenate([pw, inv], axis=0)), _bf(pw))
            yield
            pw, inv = both[0:L2], inv + both[L2:2 * L2]
        inv = inv + _dot(_bf(inv), _bf(pw))
        yield
        ps = _dot_nt(_bf(jnp.concatenate([at, rt], axis=0)), _bf(s))
        yield
        mv = _dot(_bf(jnp.concatenate([m_ak, n_rk], axis=0)), _bf(twice(v)))
        yield
        u = pick(_dot(_bf(inv), _bf(twice(ps[:L]) + mv[0:L2])))
        yield
        y_ref[sl, cols] = pick(twice(ps[L:]) + _dot(_bf(n_rb), _bf(twice(u))) + mv[L2:2 * L2])
        yield
        uv = jnp.concatenate([u, v], axis=0)
        bk = jnp.concatenate([b * g_to_end, k * g_to_end], axis=0)
        return s * jnp.exp(cum_l) + jnp.where(bd, _dot(_bf(uv.T), _bf(bk)), 0.0)

    ones = jnp.where(bd, 1.0, 0.0).astype(BF16)

    def pair(p):
        cols = slice(p * LANES, (p + 1) * LANES)
        s = s_ref[p]
        for c in range(WKV_TBLOCK // L):
            s = yield from chunk(pl.ds(c * L, L), cols, s)
            yield
        s_ref[p] = s
        y = y_ref[:, cols]
        mean = _dot_hl(y, ones) * (1.0 / RWKV_N)
        yield
        d = y - mean
        var = _dot(_bf(d * d), ones) * (1.0 / RWKV_N)
        yield
        yn = d * lax.rsqrt(var + RWKV_GN_EPS) * lnw_ref[:, cols] + lnb_ref[:, cols]
        bonus = _dot_hl(r_ref[:, cols] * k_ref[:, cols] * rk_ref[:, cols], ones) * v_ref[:, cols]
        o_ref[:, cols] = _bf((yn + bonus) * g_ref[:, cols])

    running = [pair(p) for p in range(WKV_PAIRS)]
    done = object()
    while running:
        running = [gen for gen in running if next(gen, done) is not done]


def _wkv(batch, seq, r, cum, k, v, an, bb, g, r_k, ln_w, ln_b):
    m = r.shape[0]
    nt = seq // WKV_TBLOCK
    width = WKV_PAIRS * LANES
    blk = pl.BlockSpec((WKV_TBLOCK, width), lambda b, p, i: (b * nt + i, p))
    vec = pl.BlockSpec((1, width), lambda b, p, i: (0, p))
    return pl.pallas_call(
        _wkv_kernel,
        grid=(batch, D_MODEL // width, nt),
        in_specs=[blk] * 7 + [vec] * 3,
        out_specs=blk,
        out_shape=jax.ShapeDtypeStruct((m, D_MODEL), BF16),
        scratch_shapes=[pltpu.VMEM((WKV_PAIRS, LANES, LANES), F32), pltpu.VMEM((WKV_TBLOCK, width), F32)],
        compiler_params=_params(("parallel", "parallel", "arbitrary")),
    )(r, cum, k, v, an, bb, g, r_k, ln_w, ln_b)


ATT_TQ = 256
ATT_TK = 256
NSA_TM = ATT_TK
N_POS_LANES = 3
KVA_HEAD_COLS = 4 * LANES
KVA_COLS = NSA_HKV * KVA_HEAD_COLS
KVC_COLS = NSA_HKV * LANES
QA_COLS = NSA_HKV * NSA_G * LANES
GATE_COLS = NSA_HKV * LANES
COL_PLAIN, COL_KEY_NORM, COL_POS_MOD, COL_POS_TILE, COL_ONE = range(5)


def _nsa_proj_kernel(h_ref, gn_ref, wq_ref, wkva_ref, wkvc_ref, wg_ref, qg_ref, qc_ref, kg_ref, km_ref,
                     q_o, kva_o, kvc_o, gt_o, *, tiles_per_seq):
    xb = _bf(_rms(h_ref[...], gn_ref[...]))
    bd = _bd64()
    inv_dh = 1.0 / NSA_DH
    q = _dot(xb, wq_ref[...])
    q_o[...] = _bf(q * lax.rsqrt(_seg64_sum(q * q, bd) * inv_dh + NORM_EPS) * qg_ref[...] + qc_ref[...])
    kv = _dot(xb, wkva_ref[...])
    key_group = lambda c: c % (KVA_HEAD_COLS // LANES) < 2
    kvn = kv * lax.rsqrt(_seg64_sum(kv * kv, bd, key_group) * inv_dh + NORM_EPS) * kg_ref[...]
    kind = km_ref[...]
    pos_mod = _iota(kv.shape, 0).astype(F32)
    pos_tile = (pl.program_id(0) % tiles_per_seq).astype(F32)
    kv = jnp.where(kind == COL_KEY_NORM, kvn, kv)
    kv = jnp.where(kind == COL_POS_MOD, pos_mod, kv)
    kv = jnp.where(kind == COL_POS_TILE, pos_tile, kv)
    kva_o[...] = _bf(jnp.where(kind == COL_ONE, 1.0, kv))
    kvc_o[...] = _dot(xb, wkvc_ref[...])
    gt_o[...] = jax.nn.sigmoid(_dot(xb, wg_ref[...]))


def _nsa_proj(h, seq, gn, wq, wkva, wkvc, wg, qg, qc, kg, km):
    m = h.shape[0]
    row = lambda i: (i, 0)
    return pl.pallas_call(
        functools.partial(_nsa_proj_kernel, tiles_per_seq=seq // NSA_TM),
        grid=(m // NSA_TM,),
        in_specs=[pl.BlockSpec((NSA_TM, D_MODEL), row), _const_spec((1, D_MODEL)),
                  _const_spec(wq.shape), _const_spec(wkva.shape), _const_spec(wkvc.shape), _const_spec(wg.shape),
                  _const_spec((1, QA_COLS)), _const_spec((1, QA_COLS)),
                  _const_spec((1, KVA_COLS)), _const_spec((1, KVA_COLS))],
        out_specs=[pl.BlockSpec((NSA_TM, QA_COLS), row), pl.BlockSpec((NSA_TM, KVA_COLS), row),
                   pl.BlockSpec((NSA_TM, KVC_COLS), row), pl.BlockSpec((NSA_TM, GATE_COLS), row)],
        out_shape=[jax.ShapeDtypeStruct((m, QA_COLS), BF16), jax.ShapeDtypeStruct((m, KVA_COLS), BF16),
                   jax.ShapeDtypeStruct((m, KVC_COLS), F32), jax.ShapeDtypeStruct((m, GATE_COLS), F32)],
        compiler_params=_params(("parallel",)),
    )(h, gn, wq, wkva, wkvc, wg, qg, qc, kg, km)


def _nsa_cmp_kernel(kv_ref, wa_ref, wb_ref, pa_ref, pb_ref, w2_ref, kg_ref, o_ref, *, n_blk):
    nb = n_blk + 1
    first = jnp.zeros((nb, 2 * CMP_HID), F32)
    second = jnp.zeros((nb, 2 * CMP_HID), F32)
    for p in range(CMP_STRIDE):
        x = kv_ref[pl.ds(p, nb, stride=CMP_STRIDE), :]
        first = first + _dot(_bf(x + pa_ref[p:p + 1, :]), wa_ref[p])
        second = second + _dot(_bf(x + pb_ref[p:p + 1, :]), wb_ref[p])
    hid = jax.nn.gelu(first + pltpu.roll(second, nb - 1, axis=0))
    out = _dot(_bf(hid), w2_ref[...])
    ss = _seg64_sum(out * out, _bd64()) * (1.0 / NSA_DH)
    lane = _iota((nb, 2 * LANES), 1)
    out = jnp.where(lane < NSA_DH, out * lax.rsqrt(ss + NORM_EPS) * kg_ref[...], out)
    end = _iota((nb, 2 * LANES), 0) * CMP_STRIDE + (2 * CMP_STRIDE - 1)
    sh = ATT_TK.bit_length() - 1
    out = jnp.where((lane >= NSA_DH) & (lane < NSA_DH + N_POS_LANES), (end & (ATT_TK - 1)).astype(F32), out)
    out = jnp.where((lane >= NSA_DH + N_POS_LANES) & (lane < NSA_DH + 2 * N_POS_LANES), (end >> sh).astype(F32), out)
    o_ref[...] = _bf(out)


def _nsa_cmp(batch, seq, kv, wa, wb, pa, pb, w2, kg):
    n_blk = seq // CMP_STRIDE - 1
    nb = n_blk + 1
    return pl.pallas_call(
        functools.partial(_nsa_cmp_kernel, n_blk=n_blk),
        grid=(batch, NSA_HKV),
        in_specs=[pl.BlockSpec((seq, LANES), lambda b, h: (b, h)),
                  _const_spec(wa.shape), _const_spec(wb.shape), _const_spec(pa.shape), _const_spec(pb.shape),
                  _const_spec(w2.shape), _const_spec(kg.shape)],
        out_specs=pl.BlockSpec((None, None, nb, 2 * LANES), lambda b, h: (b, h, 0, 0)),
        out_shape=jax.ShapeDtypeStruct((batch, NSA_HKV, nb, 2 * LANES), BF16),
        compiler_params=_params(("parallel", "parallel")),
    )(kv, wa, wb, pa, pb, w2, kg)


def _nsa_attn_kernel(q_ref, kv_ref, oh_ref, cmp_ref, gt_ref, o_ref,
                     qam_s, m_s, acc_s, sa_s, sb_s, ow_s, tiles_s, *, n_blk, n_sel, n_slc):
    tq, tk = ATT_TQ, ATT_TK
    R = NSA_G * tq
    qi = pl.program_id(2)
    q0 = qi * tq
    t_one = q0 + _iota((tq, 1), 0)
    gt = gt_ref[...]
    n_ch = NSA_G
    qa = jnp.concatenate([q_ref[:, g * LANES:(g + 1) * LANES] for g in range(NSA_G)], axis=0)
    qam_s[:, 0:LANES] = qa

    def window():
        n_win = WINDOW // tk + 1
        w0 = pl.multiple_of(jnp.maximum(qi - (n_win - 1), 0) * tk, tk)
        wrows = pl.ds(w0, n_win * tk)
        dist = t_one - (w0 + _iota((1, n_win * tk), 1))
        in_window = dist.astype(jnp.uint32) < WINDOW
        pes = []
        s_all = _dot_nt(qa, kv_ref[wrows, LANES:2 * LANES])
        yield
        for c in range(n_ch):
            s = jnp.where(in_window, s_all[c * tq:(c + 1) * tq], MASK_NEG)
            pes.append(_bf(jnp.exp2(s - jnp.max(s, axis=-1, keepdims=True))))
            yield
        pv = _dot(jnp.concatenate(pes, axis=0), kv_ref[wrows, 3 * LANES:4 * LANES]).reshape(NSA_G, tq, LANES)
        yield
        for g in range(NSA_G):
            ow_s[g] = (gt[:, 3 * g + 2:3 * g + 3] / pv[g][:, NSA_DH:NSA_DH + 1]) * pv[g][:, 0:NSA_DH]

    picked = {}

    def compressed():
        nb = cmp_ref.shape[0]
        n_idx = _iota((1, nb), 1)
        ps = []
        valid = (n_idx * CMP_STRIDE + (2 * CMP_STRIDE - 1) <= t_one) & (n_idx < n_blk)
        qk = _dot_nt(qa, cmp_ref[:, 0:LANES])
        yield
        for g in range(NSA_G):
            s = jnp.where(valid, qk[g * tq:(g + 1) * tq], -jnp.inf)
            mx = jnp.max(s, axis=-1, keepdims=True)
            mx = jnp.where(mx == -jnp.inf, 0.0, mx)
            e = jnp.exp2(s - mx)
            ps.append(e * (1.0 / jnp.maximum(jnp.sum(e, axis=-1, keepdims=True), 1e-30)))
            yield
        picked["o_cmp"] = _dot(_bf(jnp.concatenate(ps, axis=0)), cmp_ref[:, LANES:LANES + NSA_DH])
        yield
        psum = ps[0] + ps[1] + ps[2] + ps[3]
        sj = _iota((n_slc, nb), 0) * SLC_BLOCK
        cn = _iota((n_slc, nb), 1) * CMP_STRIDE
        overlap_t = jnp.where((cn <= sj + SLC_BLOCK - 1) & (cn + 2 * CMP_STRIDE - 1 >= sj), 1.0, 0.0).astype(BF16)
        p_hi, p_lo = _split2(psum)
        imp = _dot_nt(overlap_t, p_hi) + _dot_nt(overlap_t, p_lo)
        yield
        blk = _iota((n_slc, 1), 0)
        ahead_of_cur = ((q0 + _iota((1, tq), 1)) >> 6) - blk
        forced = (blk * ahead_of_cur * (ahead_of_cur - 1)) == 0
        imp = jnp.where(forced, SEL_BIG, imp)
        imp = jnp.where(ahead_of_cur >= 0, imp, -SEL_BIG)
        sub = 8
        ranks = []
        for lo in range(0, n_slc, sub):
            mine = imp[lo:lo + sub, :]
            rk = jnp.zeros((sub, tq), F32)
            for j in range(n_slc):
                other = imp[j:j + 1, :]
                if j < lo:
                    rk = rk + jnp.where(other >= mine, 1.0, 0.0)
                elif j >= lo + sub:
                    rk = rk + jnp.where(other > mine, 1.0, 0.0)
                else:
                    rk = rk + jnp.where(blk[lo:lo + sub] > j, jnp.where(other >= mine, 1.0, 0.0),
                                        jnp.where(other > mine, 1.0, 0.0))
            ranks.append(rk)
            yield
        rank = jnp.concatenate(ranks, axis=0)
        picked["not_sel_t"] = jnp.where(rank < n_sel, 0.0, 1.0)

    branches = {"w": window(), "c": compressed()}
    for name in "c" + "w" + "cccc" + "cc" + "wwww" + "w" + "c" * (n_slc // 8) + "cw":
        next(branches[name], None)
    assert all(next(branch, "finished") == "finished" for branch in branches.values())
    o_cmp, not_sel_t = picked["o_cmp"], picked["not_sel_t"]
    qm = jnp.concatenate([not_sel_t.T, jnp.zeros((tq, LANES - n_slc), F32)], axis=1)
    qm = _bf(jnp.where(_iota((tq, LANES), 1) == PAD_TILE_LANE, 1.0, qm))
    for g in range(NSA_G):
        qam_s[g * tq:(g + 1) * tq, LANES:2 * LANES] = qm

    n_kt = n_slc * SLC_BLOCK // tk
    in_tile = (_iota((n_kt, n_slc), 1) * SLC_BLOCK // tk) == _iota((n_kt, n_slc), 0)
    picks = _dot(jnp.where(in_tile, 1.0, 0.0).astype(BF16), _bf(1.0 - not_sel_t))
    needed = jnp.max(picks, axis=1, keepdims=True)
    n_int = jnp.int32(0)
    for kt in range(n_kt):
        tiles_s[n_int] = kt
        n_int = n_int + ((needed[kt, 0] > 0.0) & (kt < qi)).astype(jnp.int32)
    n_even = n_int + (n_int & 1)
    tiles_s[n_int] = n_kt
    tiles_s[n_even] = qi
    tiles_s[n_even + 1] = qi

    m_s[...] = jnp.full_like(m_s, MASK_NEG)
    acc_s[...] = jnp.zeros_like(acc_s)

    def key_rows(j):
        kt = tiles_s[j]
        return (pl.ds(pl.multiple_of(jnp.minimum(kt, n_kt - 1) * tk, tk), tk),
                pl.ds(pl.multiple_of(kt * tk, tk), tk))

    def scores(j, s_ref):
        kv_rows, oh_rows = key_rows(j)
        kaug = jnp.concatenate([kv_ref[kv_rows, 0:LANES], oh_ref[oh_rows, :]], axis=1)
        s_ref[...] = _dot_nt(qam_s[...], kaug)

    def softmax_pv(j, s_ref, diagonal):
        kv_rows, _ = key_rows(j)
        pes, alphas = [], []
        causal = q0 + _iota((1, tk), 1) <= t_one
        for c in range(n_ch):
            s = s_ref[c * tq:(c + 1) * tq, :]
            if diagonal:
                s = jnp.where(causal, s, MASK_NEG)
            m_old = m_s[c]
            m_new = jnp.maximum(m_old, jnp.max(s, axis=-1, keepdims=True))
            pes.append(_bf(jnp.exp2(s - jnp.concatenate([m_new] * (tk // LANES), axis=1))))
            alphas.append(jnp.exp2(m_old - m_new))
            m_s[c] = m_new
        pv = _dot(jnp.concatenate(pes, axis=0), kv_ref[kv_rows, 2 * LANES:3 * LANES])
        for c in range(n_ch):
            acc_s[c] = alphas[c] * acc_s[c] + pv[c * tq:(c + 1) * tq]

    def sel_body(jj, carry):
        j = 2 * jj
        scores(j + 1, sb_s)
        softmax_pv(j, sa_s, False)
        scores(j + 2, sa_s)
        softmax_pv(j + 1, sb_s, False)
        return carry

    scores(0, sa_s)
    lax.fori_loop(0, n_even // 2, sel_body, 0)
    softmax_pv(n_even, sa_s, True)
    acc = acc_s[...].reshape(NSA_G, tq, LANES)

    outs = []
    for g in range(NSA_G):
        w_sel = gt[:, 3 * g + 1:3 * g + 2] / acc[g][:, NSA_DH:NSA_DH + 1]
        outs.append(gt[:, 3 * g:3 * g + 1] * o_cmp[g * tq:(g + 1) * tq] + w_sel * acc[g][:, 0:NSA_DH]
                    + ow_s[g])
    o_ref[...] = _bf(jnp.concatenate(outs, axis=1))


def _nsa_attn(batch, seq, q, kva, onehot, cmp, gt):
    assert ATT_TQ == ATT_TK
    m = q.shape[0]
    nq = seq // ATT_TQ
    n_blk = seq // CMP_STRIDE - 1
    n_slc = seq // SLC_BLOCK
    n_sel = min(N_SELECT, n_slc)
    R = NSA_G * ATT_TQ
    tile = lambda b, h, i: (b * nq + i, h)
    return pl.pallas_call(
        functools.partial(_nsa_attn_kernel, n_blk=n_blk, n_sel=n_sel, n_slc=n_slc),
        grid=(batch, NSA_HKV, nq),
        in_specs=[pl.BlockSpec((ATT_TQ, NSA_G * LANES), tile),
                  pl.BlockSpec((seq, KVA_HEAD_COLS), lambda b, h, i: (b, h)),
                  pl.BlockSpec((seq + ATT_TK, LANES), lambda b, h, i: (0, 0)),
                  pl.BlockSpec((None, None, n_blk + 1, 2 * LANES), lambda b, h, i: (b, h, 0, 0)),
                  pl.BlockSpec((ATT_TQ, LANES), tile)],
        out_specs=pl.BlockSpec((ATT_TQ, NSA_G * NSA_DH), tile),
        out_shape=jax.ShapeDtypeStruct((m, D_MODEL), BF16),
        scratch_shapes=[pltpu.VMEM((R, 2 * LANES), BF16),
                        pltpu.VMEM((NSA_G, ATT_TQ, LANES), F32),
                        pltpu.VMEM((NSA_G, ATT_TQ, LANES), F32),
                        pltpu.VMEM((R, ATT_TK), F32), pltpu.VMEM((R, ATT_TK), F32),
                        pltpu.VMEM((NSA_G, ATT_TQ, NSA_DH), F32),
                        pltpu.SMEM((seq // ATT_TK + 3,), jnp.int32)],
        compiler_params=_params(("parallel", "parallel", "arbitrary")),
    )(q, kva, onehot, cmp, gt)


def _pad_cols(w, n):
    return jnp.pad(w, ((0, 0), (0, n - w.shape[1])))


def _pad_rows(w, n):
    return jnp.pad(w, ((0, n - w.shape[0]), (0, 0)))


def _lora(w_down, w_up):
    rank = -(-w_down.shape[1] // LANES) * LANES
    return _bf(_pad_cols(w_down, rank)), _bf(_pad_rows(w_up, rank))


def _rwkv_layer(h, batch, seq, gn, mu, w_in, w0, w1, w2, a0, a1, a2, g1, g2, k_k, k_a, r_k, ln_w, ln_b,
                v_first, vres):
    vec = lambda x: x.reshape(1, D_MODEL)
    w1p, w2p = _lora(w1, w2)
    a1p, a2p = _lora(a1, a2)
    g1p, g2p = _lora(g1, g2)
    if vres is not None:
        v0, v1, v2 = vres
        v1p, v2p = _lora(v1, v2)
        vres = (v_first, vec(v0), v1p, v2p)
    mu8 = jnp.pad(mu, ((0, 2), (0, 0)))
    r, lw, k, v, an, bb, g = _rwkv_prep(h, seq, vec(gn), mu8, _bf(w_in), vec(w0), w1p, w2p, vec(a0), a1p, a2p,
                                        g1p, g2p, vec(k_k), vec(k_a), vres)
    y = _wkv(batch, seq, r, lw, k, v, an, bb, g, vec(r_k), vec(ln_w), vec(ln_b))
    return y, v


def _nsa_layer(h, batch, seq, gn, w_in, cmp_pe, cmp_w1, cmp_w2, q_norm, k_norm):
    hkv, dh = NSA_HKV, NSA_DH
    nq = D_MODEL
    n_heads = hkv * NSA_G
    kv_cols = hkv * dh
    one = jnp.ones((dh,), F32)
    zero = jnp.zeros((dh,), F32)
    lane = jnp.arange(dh)
    wq = jnp.pad(w_in[:, :nq].reshape(D_MODEL, n_heads, dh), ((0, 0), (0, 0), (0, LANES - dh))).reshape(D_MODEL, QA_COLS)
    qg = jnp.tile(jnp.concatenate([q_norm * Q_PRESCALE, zero]), n_heads).reshape(1, QA_COLS)
    slopes = jnp.exp2(-8.0 * (jnp.arange(n_heads, dtype=F32) + 1.0) / n_heads) * LOG2_E
    s1 = slopes.astype(BF16).astype(F32)
    s2 = (slopes - s1).astype(BF16).astype(F32)
    s3 = (slopes - s1 - s2).astype(BF16).astype(F32)
    terms = jnp.stack([s1, s2, s3], axis=1)
    qc = jnp.concatenate([jnp.zeros((n_heads, dh), F32), terms, terms * ATT_TK,
                          jnp.zeros((n_heads, LANES - dh - 2 * N_POS_LANES), F32)], axis=1).reshape(1, QA_COLS)

    wkv = w_in[:, nq:nq + 6 * kv_cols].reshape(D_MODEL, 6, hkv, dh)
    zpad = jnp.zeros((D_MODEL, hkv, dh), F32)
    wkva = jnp.stack([wkv[:, 2], zpad, wkv[:, 4], zpad, wkv[:, 3], zpad, wkv[:, 5], zpad],
                     axis=2).reshape(D_MODEL, KVA_COLS)
    wkvc = jnp.stack([wkv[:, 0], wkv[:, 1]], axis=2).reshape(D_MODEL, KVC_COLS)
    wg = w_in[:, nq + 6 * kv_cols:].reshape(D_MODEL, hkv, 3 * NSA_G)
    wg = jnp.pad(wg, ((0, 0), (0, 0), (0, LANES - 3 * NSA_G))).reshape(D_MODEL, GATE_COLS)
    pos_kind = jnp.where(lane < N_POS_LANES, COL_POS_MOD, jnp.where(lane < 2 * N_POS_LANES, COL_POS_TILE, COL_PLAIN))
    one_kind = jnp.where(lane == 0, COL_ONE, COL_PLAIN)
    key_kind = jnp.full((dh,), COL_KEY_NORM)
    plain = jnp.full((dh,), COL_PLAIN)
    kg = jnp.tile(jnp.concatenate([k_norm[1], one, k_norm[2], one, one, one, one, one]), hkv).reshape(1, KVA_COLS)
    km = jnp.tile(jnp.concatenate([key_kind, pos_kind, key_kind, pos_kind, plain, one_kind, plain, one_kind]),
                  hkv).reshape(1, KVA_COLS).astype(jnp.int32)
    q, kva, kvc, gt = _nsa_proj(h, seq, gn.reshape(1, D_MODEL), _bf(wq), _bf(wkva), _bf(wkvc), _bf(wg),
                                qg, qc, kg, km)

    w1 = cmp_w1.reshape(2, 2 * CMP_STRIDE, dh, CMP_HID)
    zeros = jnp.zeros((2 * CMP_STRIDE, dh, CMP_HID), F32)
    wfull = jnp.concatenate([jnp.concatenate([w1[0], zeros], axis=2),
                             jnp.concatenate([zeros, w1[1]], axis=2)], axis=1)
    pe = jnp.concatenate([cmp_pe[0], cmp_pe[1]], axis=1)
    w2 = jnp.zeros((2 * CMP_HID, 2 * LANES), F32)
    w2 = w2.at[:CMP_HID, :dh].set(cmp_w2[0]).at[CMP_HID:, LANES:LANES + dh].set(cmp_w2[1])
    kg0 = jnp.concatenate([k_norm[0], one, one, one]).reshape(1, 2 * LANES)
    cmp = _nsa_cmp(batch, seq, kvc, _bf(wfull[:CMP_STRIDE]), _bf(wfull[CMP_STRIDE:]),
                   pe[:CMP_STRIDE], pe[CMP_STRIDE:], _bf(w2), kg0)

    key = jnp.arange(seq + ATT_TK)[:, None]
    lanes = jnp.arange(LANES)[None, :]
    onehot = jnp.where(key < seq, lanes == key // SLC_BLOCK, lanes == PAD_TILE_LANE)
    return _nsa_attn(batch, seq, q, kva, _bf(jnp.where(onehot, MASK_NEG, 0.0)), cmp, gt)


def kernel(x, mix_norm, mlp_norm, mlp_w1, mlp_w2, rwkv_mu, rwkv_w_in, rwkv_w0, rwkv_w1, rwkv_w2, rwkv_a0, rwkv_a1, rwkv_a2, rwkv_v0, rwkv_v1, rwkv_v2, rwkv_g1, rwkv_g2, rwkv_k_k, rwkv_k_a, rwkv_r_k, rwkv_ln_w, rwkv_ln_b, rwkv_w_out, nsa_w_in, nsa_cmp_pe, nsa_cmp_w1, nsa_cmp_w2, nsa_q_norm, nsa_k_norm, nsa_w_out):
    batch, seq, d = x.shape
    depth = mix_norm.shape[0]
    h = x.reshape(batch * seq, d)
    v_first = None
    for i in range(depth):
        j = i // 2
        if i % 2 == 0:
            vres = None if j == 0 else (rwkv_v0[j - 1], rwkv_v1[j - 1], rwkv_v2[j - 1])
            a, v_raw = _rwkv_layer(h, batch, seq, mix_norm[i], rwkv_mu[j], rwkv_w_in[j], rwkv_w0[j], rwkv_w1[j],
                                   rwkv_w2[j], rwkv_a0[j], rwkv_a1[j], rwkv_a2[j], rwkv_g1[j], rwkv_g2[j],
                                   rwkv_k_k[j], rwkv_k_a[j], rwkv_r_k[j].reshape(-1), rwkv_ln_w[j], rwkv_ln_b[j],
                                   v_first, vres)
            if j == 0:
                v_first = v_raw
            wo = rwkv_w_out[j]
        else:
            a = _nsa_layer(h, batch, seq, mix_norm[i], nsa_w_in[j], nsa_cmp_pe[j], nsa_cmp_w1[j], nsa_cmp_w2[j],
                           nsa_q_norm[j], nsa_k_norm[j])
            wo = nsa_w_out[j]
        h = _out_mlp(h, a, _bf(wo), mlp_norm[i].reshape(1, d), _bf(mlp_w1[i]), _bf(mlp_w2[i]))
    return h.reshape(batch, seq, d)
```

```python
import functools

import jax
import jax.numpy as jnp
import numpy as np
from jax import lax
from jax.experimental import pallas as pl
from jax.experimental.pallas import tpu as pltpu

F32 = jnp.float32
BF16 = jnp.bfloat16

D_MODEL = 1024
MLP_HIDDEN = 4 * D_MODEL
NORM_EPS = 1e-6
LANES = 128
VMEM_LIMIT = 56 * 1024 * 1024

RWKV_N = 64
RWKV_GN_EPS = 64e-5
WKV_CHUNK = 64
WKV_TBLOCK = 128
WKV_PAIRS = 8

NSA_HKV = 4
NSA_G = 4
NSA_DH = 64
CMP_STRIDE = 16
CMP_HID = 128
SLC_BLOCK = 64
N_SELECT = 16
WINDOW = 512
SEL_BIG = 1e9
MASK_NEG = -1e30
LOG2_E = 1.4426950408889634
Q_PRESCALE = NSA_DH ** -0.5 * LOG2_E
PAD_TILE_LANE = LANES - 1


def _dot(a, b):
    return jnp.dot(a, b, preferred_element_type=F32)


def _dot_nt(a, b):
    return lax.dot_general(a, b, (((1,), (1,)), ((), ())), preferred_element_type=F32)


def _bf(x):
    return x.astype(BF16)


def _split2(x):
    hi = x.astype(BF16)
    lo = (x - hi.astype(F32)).astype(BF16)
    return hi, lo


def _dot_hl(x, w):
    hi, lo = _split2(x)
    return _dot(hi, w) + _dot(lo, w)


def _dot_hl_rev(w, x):
    hi, lo = _split2(x)
    return _dot(w, hi) + _dot(w, lo)


def _iota(shape, dim):
    return lax.broadcasted_iota(jnp.int32, shape, dim)


def _bd64(n=LANES):
    return jnp.where((_iota((n, n), 0) >> 6) == (_iota((n, n), 1) >> 6), 1.0, 0.0).astype(BF16)


def _seg64_sum(x, bd, keep=lambda c: True):
    outs = [_dot(_bf(x[:, c * LANES:(c + 1) * LANES]), bd) if keep(c) else
            jnp.zeros((x.shape[0], LANES), F32) for c in range(x.shape[1] // LANES)]
    return outs[0] if len(outs) == 1 else jnp.concatenate(outs, axis=1)


def _rms(x, g):
    return x * lax.rsqrt(jnp.mean(x * x, axis=-1, keepdims=True) + NORM_EPS) * g


def _const_spec(shape):
    return pl.BlockSpec(shape, lambda *_: (0,) * len(shape))


def _params(sem):
    return pltpu.CompilerParams(dimension_semantics=sem, vmem_limit_bytes=VMEM_LIMIT)


MLP_TM = 512
MLP_TH = 1024


def _out_mlp_kernel(h_ref, a_ref, wo_ref, g_ref, w1_ref, w2_ref, o_ref):
    h1 = h_ref[...] + _dot(a_ref[...], wo_ref[...])
    xb = _bf(_rms(h1, g_ref[...]))
    acc = h1
    for c in range(MLP_HIDDEN // MLP_TH):
        u = jnp.maximum(_dot(xb, w1_ref[:, c * MLP_TH:(c + 1) * MLP_TH]), 0.0)
        acc = acc + _dot(_bf(u * u), w2_ref[c * MLP_TH:(c + 1) * MLP_TH, :])
    o_ref[...] = acc


def _out_mlp(h, a, wo, g, w1, w2):
    m = h.shape[0]
    row = lambda i: (i, 0)
    return pl.pallas_call(
        _out_mlp_kernel,
        grid=(m // MLP_TM,),
        in_specs=[pl.BlockSpec((MLP_TM, D_MODEL), row), pl.BlockSpec((MLP_TM, D_MODEL), row),
                  _const_spec((D_MODEL, D_MODEL)), _const_spec((1, D_MODEL)),
                  _const_spec((D_MODEL, MLP_HIDDEN)), _const_spec((MLP_HIDDEN, D_MODEL))],
        out_specs=pl.BlockSpec((MLP_TM, D_MODEL), row),
        out_shape=jax.ShapeDtypeStruct((m, D_MODEL), F32),
        compiler_params=_params(("parallel",)),
    )(h, a, wo, g, w1, w2)


RW_TM = 256
SHIFT_ROWS = 8


def _rwkv_prep_kernel(*refs, tiles_per_seq, vres):
    if vres:
        (h_ref, hp_ref, gn_ref, mu_ref, win_ref, w0_ref, w1_ref, w2_ref, a0_ref, a1_ref, a2_ref,
         g1_ref, g2_ref, kk_ref, ka_ref, vf_ref, v0_ref, v1_ref, v2_ref,
         r_o, lw_o, k_o, v_o, an_o, bb_o, g_o) = refs
    else:
        (h_ref, hp_ref, gn_ref, mu_ref, win_ref, w0_ref, w1_ref, w2_ref, a0_ref, a1_ref, a2_ref,
         g1_ref, g2_ref, kk_ref, ka_ref,
         r_o, lw_o, k_o, v_o, an_o, bb_o, g_o) = refs
    gn = gn_ref[...]
    xn = _rms(h_ref[...], gn)
    first = (pl.program_id(0) % tiles_per_seq) == 0
    prev = _rms(hp_ref[SHIFT_ROWS - 1:SHIFT_ROWS, :], gn)
    prev = jnp.where(first, 0.0, prev)
    xs = pltpu.roll(xn, 1, axis=0)
    xs = jnp.where(_iota(xn.shape, 0) == 0, prev, xs)
    dx = xs - xn
    xr, xw, xk, xv, xa, xg = (xn + dx * mu_ref[i:i + 1, :] for i in range(6))
    r = _dot(_bf(xr), win_ref[:, 0:D_MODEL])
    k = _dot(_bf(xk), win_ref[:, D_MODEL:2 * D_MODEL])
    v = _dot(_bf(xv), win_ref[:, 2 * D_MODEL:3 * D_MODEL])
    if vres:
        lo = _dot(_bf(_dot(_bf(xv), v1_ref[...])), v2_ref[...])
        v = v + (vf_ref[...] - v) * jax.nn.sigmoid(v0_ref[...] + lo)
    z = w0_ref[...] + _dot(_bf(jnp.tanh(_dot(_bf(xw), w1_ref[...]))), w2_ref[...])
    softplus = jnp.maximum(-z, 0.0) + jnp.log(1.0 + jnp.exp(-jnp.abs(z)))
    lw = -jnp.exp(-softplus - 0.5)
    a = jax.nn.sigmoid(a0_ref[...] + _dot(_bf(_dot(_bf(xa), a1_ref[...])), a2_ref[...]))
    g = _dot(_bf(jax.nn.sigmoid(_dot(_bf(xg), g1_ref[...]))), g2_ref[...])
    kk = k * kk_ref[...]
    nrm = jnp.sqrt(_seg64_sum(kk * kk, _bd64()))
    kk = kk / jnp.maximum(nrm, 1e-12)
    ii = _iota((RW_TM, RW_TM), 0)
    jj = _iota((RW_TM, RW_TM), 1)
    sh = WKV_CHUNK.bit_length() - 1
    cum_mat = jnp.where(((ii >> sh) == (jj >> sh)) & (jj <= ii), 1.0, 0.0).astype(BF16)
    r_o[...] = r
    lw_o[...] = _dot_hl_rev(cum_mat, lw)
    k_o[...] = k * (1.0 + (a - 1.0) * ka_ref[...])
    v_o[...] = v
    an_o[...] = -kk
    bb_o[...] = kk * a
    g_o[...] = g


def _rwkv_prep(h, seq, gn, mu, win, w0, w1, w2, a0, a1, a2, g1, g2, k_k, k_a, vres):
    m = h.shape[0]
    row = lambda i: (i, 0)
    prev = lambda i: (jnp.maximum(i * (RW_TM // SHIFT_ROWS) - 1, 0), 0)
    vec = _const_spec((1, D_MODEL))
    full = lambda a: _const_spec(a.shape)
    args = [h, h, gn, mu, win, w0, w1, w2, a0, a1, a2, g1, g2, k_k, k_a]
    specs = [pl.BlockSpec((RW_TM, D_MODEL), row), pl.BlockSpec((SHIFT_ROWS, D_MODEL), prev), vec, full(mu),
             full(win), vec, full(w1), full(w2), vec, full(a1), full(a2), full(g1), full(g2), vec, vec]
    if vres is not None:
        v_first, v0, v1, v2 = vres
        args += [v_first, v0, v1, v2]
        specs += [pl.BlockSpec((RW_TM, D_MODEL), row), vec, full(v1), full(v2)]
    out = jax.ShapeDtypeStruct((m, D_MODEL), F32)
    return pl.pallas_call(
        functools.partial(_rwkv_prep_kernel, tiles_per_seq=seq // RW_TM, vres=vres is not None),
        grid=(m // RW_TM,),
        in_specs=specs,
        out_specs=[pl.BlockSpec((RW_TM, D_MODEL), row)] * 7,
        out_shape=[out] * 7,
        compiler_params=_params(("parallel",)),
    )(*args)


def _wkv_kernel(r_ref, cum_ref, k_ref, v_ref, a_ref, b_ref, g_ref, rk_ref, lnw_ref, lnb_ref,
                o_ref, s_ref, y_ref):
    L = WKV_CHUNK
    L2 = 2 * L

    @pl.when(pl.program_id(2) == 0)
    def _():
        s_ref[...] = jnp.zeros_like(s_ref)

    head0 = _iota((1, LANES), 1) < RWKV_N
    ii = _iota((L2, L2), 0)
    jj = _iota((L2, L2), 1)
    same = (ii >= L) == (jj >= L)
    tril_strict = same & (jj < ii)
    tril_incl = same & (jj <= ii)
    eye = jnp.where(ii == jj, 1.0, 0.0)
    first_row = _iota((L, LANES), 0) == 0
    bd = (_iota((LANES, LANES), 0) >> 6) == (_iota((LANES, LANES), 1) >> 6)

    def by_head(x):
        return jnp.concatenate([jnp.where(head0, x, 0.0), jnp.where(head0, 0.0, x)], axis=0)

    def twice(x):
        return jnp.concatenate([x, x], axis=0)

    def pick(x2):
        return jnp.where(head0, x2[:L], x2[L:])

    def chunk(sl, cols, s):
        cum = cum_ref[sl, cols]
        cum_ex = jnp.where(first_row, 0.0, pltpu.roll(cum, 1, axis=0))
        g_in = jnp.exp(cum)
        g_inv = jnp.exp(-cum)
        cum_l = cum[L - 1:L, :]
        g_to_end = jnp.exp(cum_l - cum)
        r = r_ref[sl, cols]
        k = k_ref[sl, cols]
        v = v_ref[sl, cols]
        b = b_ref[sl, cols]
        at = a_ref[sl, cols] * jnp.exp(cum_ex)
        rt = r * g_in
        bt = b * g_inv
        kt = k * g_inv
        sc = _dot_nt(_bf(jnp.concatenate([by_head(at), by_head(rt)], axis=0)),
                     _bf(jnp.concatenate([bt, bt, kt, kt], axis=0)))
        yield
        m_ab = jnp.where(tril_strict, sc[0:L2, 0:L2], 0.0)
        m_ak = jnp.where(tril_strict, sc[0:L2, L2:2 * L2], 0.0)
        n_rb = jnp.where(tril_incl, sc[L2:2 * L2, 0:L2], 0.0)
        n_rk = jnp.where(tril_incl, sc[L2:2 * L2, L2:2 * L2], 0.0)
        inv = eye + m_ab
        pw = _dot(_bf(m_ab), _bf(m_ab))
        yield
        for _ in range(int(np.log2(L)) - 2):
            both = _dot(_bf(jnp.concatenate([pw, inv], axis=0)), _bf(pw))
            yield
            pw, inv = both[0:L2], inv + both[L2:2 * L2]
        inv = inv + _dot(_bf(inv), _bf(pw))
        yield
        ps = _dot_nt(_bf(jnp.concatenate([at, rt], axis=0)), _bf(s))
        yield
        mv = _dot(_bf(jnp.concatenate([m_ak, n_rk], axis=0)), _bf(twice(v)))
        yield
        u = pick(_dot(_bf(inv), _bf(twice(ps[:L]) + mv[0:L2])))
        yield
        y_ref[sl, cols] = pick(twice(ps[L:]) + _dot(_bf(n_rb), _bf(twice(u))) + mv[L2:2 * L2])
        yield
        uv = jnp.concatenate([u, v], axis=0)
        bk = jnp.concatenate([b * g_to_end, k * g_to_end], axis=0)
        return s * jnp.exp(cum_l) + jnp.where(bd, _dot(_bf(uv.T), _bf(bk)), 0.0)

    ones = jnp.where(bd, 1.0, 0.0).astype(BF16)

    def pair(p):
        cols = slice(p * LANES, (p + 1) * LANES)
        s = s_ref[p]
        for c in range(WKV_TBLOCK // L):
            s = yield from chunk(pl.ds(c * L, L), cols, s)
            yield
        s_ref[p] = s
        y = y_ref[:, cols]
        mean = _dot_hl(y, ones) * (1.0 / RWKV_N)
        yield
        d = y - mean
        var = _dot(_bf(d * d), ones) * (1.0 / RWKV_N)
        yield
        yn = d * lax.rsqrt(var + RWKV_GN_EPS) * lnw_ref[:, cols] + lnb_ref[:, cols]
        bonus = _dot_hl(r_ref[:, cols] * k_ref[:, cols] * rk_ref[:, cols], ones) * v_ref[:, cols]
        o_ref[:, cols] = _bf((yn + bonus) * g_ref[:, cols])

    running = [pair(p) for p in range(WKV_PAIRS)]
    done = object()
    while running:
        running = [gen for gen in running if next(gen, done) is not done]


def _wkv(batch, seq, r, cum, k, v, an, bb, g, r_k, ln_w, ln_b):
    m = r.shape[0]
    nt = seq // WKV_TBLOCK
    width = WKV_PAIRS * LANES
    blk = pl.BlockSpec((WKV_TBLOCK, width), lambda b, p, i: (b * nt + i, p))
    vec = pl.BlockSpec((1, width), lambda b, p, i: (0, p))
    return pl.pallas_call(
        _wkv_kernel,
        grid=(batch, D_MODEL // width, nt),
        in_specs=[blk] * 7 + [vec] * 3,
        out_specs=blk,
        out_shape=jax.ShapeDtypeStruct((m, D_MODEL), BF16),
        scratch_shapes=[pltpu.VMEM((WKV_PAIRS, LANES, LANES), F32), pltpu.VMEM((WKV_TBLOCK, width), F32)],
        compiler_params=_params(("parallel", "parallel", "arbitrary")),
    )(r, cum, k, v, an, bb, g, r_k, ln_w, ln_b)


ATT_TQ = 256
ATT_TK = 256
NSA_TM = ATT_TK
N_POS_LANES = 3
KVA_HEAD_COLS = 4 * LANES
KVA_COLS = NSA_HKV * KVA_HEAD_COLS
KVC_COLS = NSA_HKV * LANES
QA_COLS = NSA_HKV * NSA_G * LANES
GATE_COLS = NSA_HKV * LANES


def _nsa_proj_kernel(h_ref, gn_ref, wq_ref, wkv_ref, wkvc_ref, wg_ref, qg_ref, qc_ref, kg_ref,
                     q_o, kva_o, kvc_o, gt_o, *, tiles_per_seq):
    xb = _bf(_rms(h_ref[...], gn_ref[...]))
    bd = _bd64()
    inv_dh = 1.0 / NSA_DH
    lane = _iota((NSA_TM, LANES), 1)
    low = lane < NSA_DH

    def spread(pair, fill_a, fill_b):
        return [jnp.where(low, pair, fill_a), jnp.where(low, pltpu.roll(pair, NSA_DH, axis=1), fill_b)]

    q = _dot(xb, wq_ref[...])
    q = q * lax.rsqrt(_seg64_sum(q * q, bd) * inv_dh + NORM_EPS) * qg_ref[...]
    outs = []
    for c in range(D_MODEL // LANES):
        outs += spread(q[:, c * LANES:(c + 1) * LANES], qc_ref[:, 2 * c * LANES:(2 * c + 1) * LANES],
                       qc_ref[:, (2 * c + 1) * LANES:(2 * c + 2) * LANES])
    q_o[...] = _bf(jnp.concatenate(outs, axis=1))

    pos_mod = _iota((NSA_TM, LANES), 0).astype(F32)
    pos_tile = (pl.program_id(0) % tiles_per_seq).astype(F32)
    key_fill = jnp.where(lane < NSA_DH + N_POS_LANES, pos_mod, jnp.where(lane < NSA_DH + 2 * N_POS_LANES, pos_tile, 0.0))
    val_fill = jnp.where(lane == NSA_DH, 1.0, 0.0)
    kv = _dot(xb, wkv_ref[...])
    outs = []
    for hd in range(NSA_HKV):
        keys = kv[:, 2 * hd * LANES:(2 * hd + 1) * LANES]
        keys = keys * lax.rsqrt(_dot(_bf(keys * keys), bd) * inv_dh + NORM_EPS) * kg_ref[:, hd * LANES:(hd + 1) * LANES]
        outs += spread(keys, key_fill, key_fill)
        outs += spread(kv[:, (2 * hd + 1) * LANES:(2 * hd + 2) * LANES], val_fill, val_fill)
    kva_o[...] = _bf(jnp.concatenate(outs, axis=1))

    kvc_o[...] = _dot(xb, wkvc_ref[...])
    gates = jax.nn.sigmoid(_dot(xb, wg_ref[...]))
    per_head = 3 * NSA_G
    gt_o[...] = jnp.concatenate([gates if hd == 0 else pltpu.roll(gates, LANES - per_head * hd, axis=1)
                                 for hd in range(NSA_HKV)], axis=1)


def _nsa_proj(h, seq, gn, wq, wkv, wkvc, wg, qg, qc, kg):
    m = h.shape[0]
    row = lambda i: (i, 0)
    return pl.pallas_call(
        functools.partial(_nsa_proj_kernel, tiles_per_seq=seq // NSA_TM),
        grid=(m // NSA_TM,),
        in_specs=[pl.BlockSpec((NSA_TM, D_MODEL), row), _const_spec((1, D_MODEL)),
                  _const_spec(wq.shape), _const_spec(wkv.shape), _const_spec(wkvc.shape), _const_spec(wg.shape),
                  _const_spec(qg.shape), _const_spec(qc.shape), _const_spec(kg.shape)],
        out_specs=[pl.BlockSpec((NSA_TM, QA_COLS), row), pl.BlockSpec((NSA_TM, KVA_COLS), row),
                   pl.BlockSpec((NSA_TM, KVC_COLS), row), pl.BlockSpec((NSA_TM, GATE_COLS), row)],
        out_shape=[jax.ShapeDtypeStruct((m, QA_COLS), BF16), jax.ShapeDtypeStruct((m, KVA_COLS), BF16),
                   jax.ShapeDtypeStruct((m, KVC_COLS), F32), jax.ShapeDtypeStruct((m, GATE_COLS), F32)],
        compiler_params=_params(("parallel",)),
    )(h, gn, wq, wkv, wkvc, wg, qg, qc, kg)


def _nsa_cmp_kernel(kv_ref, wa_ref, wb_ref, pa_ref, pb_ref, w2_ref, kg_ref, o_ref, *, n_blk):
    nb = n_blk + 1
    first = jnp.zeros((nb, 2 * CMP_HID), F32)
    second = jnp.zeros((nb, 2 * CMP_HID), F32)
    for p in range(CMP_STRIDE):
        x = kv_ref[pl.ds(p, nb, stride=CMP_STRIDE), :]
        first = first + _dot(_bf(x + pa_ref[p:p + 1, :]), wa_ref[p])
        second = second + _dot(_bf(x + pb_ref[p:p + 1, :]), wb_ref[p])
    hid = jax.nn.gelu(first + pltpu.roll(second, nb - 1, axis=0))
    out = _dot(_bf(hid), w2_ref[...])
    ss = _seg64_sum(out * out, _bd64()) * (1.0 / NSA_DH)
    lane = _iota((nb, 2 * LANES), 1)
    out = jnp.where(lane < NSA_DH, out * lax.rsqrt(ss + NORM_EPS) * kg_ref[...], out)
    end = _iota((nb, 2 * LANES), 0) * CMP_STRIDE + (2 * CMP_STRIDE - 1)
    sh = ATT_TK.bit_length() - 1
    out = jnp.where((lane >= NSA_DH) & (lane < NSA_DH + N_POS_LANES), (end & (ATT_TK - 1)).astype(F32), out)
    out = jnp.where((lane >= NSA_DH + N_POS_LANES) & (lane < NSA_DH + 2 * N_POS_LANES), (end >> sh).astype(F32), out)
    o_ref[...] = _bf(out)


def _nsa_cmp(batch, seq, kv, wa, wb, pa, pb, w2, kg):
    n_blk = seq // CMP_STRIDE - 1
    nb = n_blk + 1
    return pl.pallas_call(
        functools.partial(_nsa_cmp_kernel, n_blk=n_blk),
        grid=(batch, NSA_HKV),
        in_specs=[pl.BlockSpec((seq, LANES), lambda b, h: (b, h)),
                  _const_spec(wa.shape), _const_spec(wb.shape), _const_spec(pa.shape), _const_spec(pb.shape),
                  _const_spec(w2.shape), _const_spec(kg.shape)],
        out_specs=pl.BlockSpec((None, None, nb, 2 * LANES), lambda b, h: (b, h, 0, 0)),
        out_shape=jax.ShapeDtypeStruct((batch, NSA_HKV, nb, 2 * LANES), BF16),
        compiler_params=_params(("parallel", "parallel")),
    )(kv, wa, wb, pa, pb, w2, kg)


def _nsa_attn_kernel(q_ref, kv_ref, oh_ref, cmp_ref, gt_ref, o_ref,
                     qam_s, m_s, acc_s, sa_s, sb_s, ow_s, tiles_s, *, n_blk, n_sel, n_slc):
    tq, tk = ATT_TQ, ATT_TK
    R = NSA_G * tq
    qi = pl.program_id(2)
    q0 = qi * tq
    t_one = q0 + _iota((tq, 1), 0)
    gt = gt_ref[...]
    n_ch = NSA_G
    qa = jnp.concatenate([q_ref[:, g * LANES:(g + 1) * LANES] for g in range(NSA_G)], axis=0)
    qam_s[:, 0:LANES] = qa

    n_win = WINDOW // tk + 1
    wrows = pl.ds(pl.multiple_of(jnp.maximum(qi - (n_win - 1), 0) * tk, tk), n_win * tk)

    def window():
        dist = t_one - (wrows.start + _iota((1, n_win * tk), 1))
        in_window = dist.astype(jnp.uint32) < WINDOW
        s_all = _dot_nt(qa, kv_ref[wrows, LANES:2 * LANES])
        yield
        pes = []
        for c in range(n_ch):
            s = jnp.where(in_window, s_all[c * tq:(c + 1) * tq], MASK_NEG)
            pes.append(_bf(jnp.exp2(s - jnp.max(s, axis=-1, keepdims=True))))
            yield
        pv = _dot(jnp.concatenate(pes, axis=0), kv_ref[wrows, 3 * LANES:4 * LANES]).reshape(NSA_G, tq, LANES)
        yield
        for g in range(NSA_G):
            ow_s[g] = (gt[:, 3 * g + 2:3 * g + 3] / pv[g][:, NSA_DH:NSA_DH + 1]) * pv[g][:, 0:NSA_DH]

    picked = {}

    def compressed():
        nb = cmp_ref.shape[0]
        n_idx = _iota((1, nb), 1)
        ps = []
        valid = (n_idx * CMP_STRIDE + (2 * CMP_STRIDE - 1) <= t_one) & (n_idx < n_blk)
        qk = _dot_nt(qa, cmp_ref[:, 0:LANES])
        yield
        for g in range(NSA_G):
            s = jnp.where(valid, qk[g * tq:(g + 1) * tq], -jnp.inf)
            mx = jnp.max(s, axis=-1, keepdims=True)
            mx = jnp.where(mx == -jnp.inf, 0.0, mx)
            e = jnp.exp2(s - mx)
            ps.append(e * (1.0 / jnp.maximum(jnp.sum(e, axis=-1, keepdims=True), 1e-30)))
            yield
        picked["o_cmp"] = _dot(_bf(jnp.concatenate(ps, axis=0)), cmp_ref[:, LANES:LANES + NSA_DH])
        yield
        psum = ps[0] + ps[1] + ps[2] + ps[3]
        sj = _iota((n_slc, nb), 0) * SLC_BLOCK
        cn = _iota((n_slc, nb), 1) * CMP_STRIDE
        overlap_t = jnp.where((cn <= sj + SLC_BLOCK - 1) & (cn + 2 * CMP_STRIDE - 1 >= sj), 1.0, 0.0).astype(BF16)
        p_hi, p_lo = _split2(psum)
        imp = _dot_nt(overlap_t, p_hi) + _dot_nt(overlap_t, p_lo)
        yield
        blk = _iota((n_slc, 1), 0)
        ahead_of_cur = ((q0 + _iota((1, tq), 1)) >> 6) - blk
        forced = (blk * ahead_of_cur * (ahead_of_cur - 1)) == 0
        imp = jnp.where(forced, SEL_BIG, imp)
        imp = jnp.where(ahead_of_cur >= 0, imp, -SEL_BIG)
        sub = 8
        ranks = []
        for lo in range(0, n_slc, sub):
            mine = imp[lo:lo + sub, :]
            rk = jnp.zeros((sub, tq), F32)
            for j in range(n_slc):
                other = imp[j:j + 1, :]
                if j < lo:
                    rk = rk + jnp.where(other >= mine, 1.0, 0.0)
                elif j >= lo + sub:
                    rk = rk + jnp.where(other > mine, 1.0, 0.0)
                else:
                    rk = rk + jnp.where(blk[lo:lo + sub] > j, jnp.where(other >= mine, 1.0, 0.0),
                                        jnp.where(other > mine, 1.0, 0.0))
            ranks.append(rk)
            yield
        rank = jnp.concatenate(ranks, axis=0)
        picked["not_sel_t"] = jnp.where(rank < n_sel, 0.0, 1.0)

    branches = {"w": window(), "c": compressed()}
    for name in "c" + "w" + "cccc" + "cc" + "wwww" + "w" + "c" * (n_slc // 8) + "cw":
        next(branches[name], None)
    assert all(next(branch, "finished") == "finished" for branch in branches.values())
    o_cmp, not_sel_t = picked["o_cmp"], picked["not_sel_t"]
    qm = jnp.concatenate([not_sel_t.T, jnp.zeros((tq, LANES - n_slc), F32)], axis=1)
    qm = _bf(jnp.where(_iota((tq, LANES), 1) == PAD_TILE_LANE, 1.0, qm))
    for g in range(NSA_G):
        qam_s[g * tq:(g + 1) * tq, LANES:2 * LANES] = qm

    n_kt = n_slc * SLC_BLOCK // tk
    in_tile = (_iota((n_kt, n_slc), 1) * SLC_BLOCK // tk) == _iota((n_kt, n_slc), 0)
    picks = _dot(jnp.where(in_tile, 1.0, 0.0).astype(BF16), _bf(1.0 - not_sel_t))
    needed = jnp.max(picks, axis=1, keepdims=True)
    n_int = jnp.int32(0)
    for kt in range(n_kt):
        tiles_s[n_int] = kt
        n_int = n_int + ((needed[kt, 0] > 0.0) & (kt < qi)).astype(jnp.int32)
    n_even = n_int + (n_int & 1)
    tiles_s[n_int] = n_kt
    tiles_s[n_even] = qi
    tiles_s[n_even + 1] = qi

    m_s[...] = jnp.full_like(m_s, MASK_NEG)
    acc_s[...] = jnp.zeros_like(acc_s)

    def key_rows(j):
        kt = tiles_s[j]
        return (pl.ds(pl.multiple_of(jnp.minimum(kt, n_kt - 1) * tk, tk), tk),
                pl.ds(pl.multiple_of(kt * tk, tk), tk))

    def scores(j, s_ref):
        kv_rows, oh_rows = key_rows(j)
        kaug = jnp.concatenate([kv_ref[kv_rows, 0:LANES], oh_ref[oh_rows, :]], axis=1)
        s_ref[...] = _dot_nt(qam_s[...], kaug)

    def softmax_pv(j, s_ref, diagonal):
        kv_rows, _ = key_rows(j)
        vb = kv_ref[kv_rows, 2 * LANES:3 * LANES]
        causal = q0 + _iota((1, tk), 1) <= t_one
        for c in range(n_ch):
            s = s_ref[c * tq:(c + 1) * tq, :]
            if diagonal:
                s = jnp.where(causal, s, MASK_NEG)
            m_old = m_s[c]
            m_new = jnp.maximum(m_old, jnp.max(s, axis=-1, keepdims=True))
            pe = _bf(jnp.exp2(s - jnp.concatenate([m_new] * (tk // LANES), axis=1)))
            acc_s[c] = jnp.exp2(m_old - m_new) * acc_s[c] + _dot(pe, vb)
            m_s[c] = m_new

    def sel_body(jj, carry):
        j = 2 * jj
        scores(j + 1, sb_s)
        softmax_pv(j, sa_s, False)
        scores(j + 2, sa_s)
        softmax_pv(j + 1, sb_s, False)
        return carry

    scores(0, sa_s)
    lax.fori_loop(0, n_even // 2, sel_body, 0)
    softmax_pv(n_even, sa_s, True)
    acc = acc_s[...].reshape(NSA_G, tq, LANES)

    outs = []
    for g in range(NSA_G):
        w_sel = gt[:, 3 * g + 1:3 * g + 2] / acc[g][:, NSA_DH:NSA_DH + 1]
        outs.append(gt[:, 3 * g:3 * g + 1] * o_cmp[g * tq:(g + 1) * tq] + w_sel * acc[g][:, 0:NSA_DH]
                    + ow_s[g])
    o_ref[...] = _bf(jnp.concatenate(outs, axis=1))


def _nsa_attn(batch, seq, q, kva, onehot, cmp, gt):
    assert ATT_TQ == ATT_TK
    m = q.shape[0]
    nq = seq // ATT_TQ
    n_blk = seq // CMP_STRIDE - 1
    n_slc = seq // SLC_BLOCK
    n_sel = min(N_SELECT, n_slc)
    R = NSA_G * ATT_TQ
    tile = lambda b, h, i: (b * nq + i, h)
    return pl.pallas_call(
        functools.partial(_nsa_attn_kernel, n_blk=n_blk, n_sel=n_sel, n_slc=n_slc),
        grid=(batch, NSA_HKV, nq),
        in_specs=[pl.BlockSpec((ATT_TQ, NSA_G * LANES), tile),
                  pl.BlockSpec((seq, KVA_HEAD_COLS), lambda b, h, i: (b, h)),
                  pl.BlockSpec((seq + ATT_TK, LANES), lambda b, h, i: (0, 0)),
                  pl.BlockSpec((None, None, n_blk + 1, 2 * LANES), lambda b, h, i: (b, h, 0, 0)),
                  pl.BlockSpec((ATT_TQ, LANES), tile)],
        out_specs=pl.BlockSpec((ATT_TQ, NSA_G * NSA_DH), tile),
        out_shape=jax.ShapeDtypeStruct((m, D_MODEL), BF16),
        scratch_shapes=[pltpu.VMEM((R, 2 * LANES), BF16),
                        pltpu.VMEM((NSA_G, ATT_TQ, LANES), F32),
                        pltpu.VMEM((NSA_G, ATT_TQ, LANES), F32),
                        pltpu.VMEM((R, ATT_TK), F32), pltpu.VMEM((R, ATT_TK), F32),
                        pltpu.VMEM((NSA_G, ATT_TQ, NSA_DH), F32),
                        pltpu.SMEM((seq // ATT_TK + 3,), jnp.int32)],
        compiler_params=_params(("parallel", "parallel", "arbitrary")),
    )(q, kva, onehot, cmp, gt)


def _pad_cols(w, n):
    return jnp.pad(w, ((0, 0), (0, n - w.shape[1])))


def _pad_rows(w, n):
    return jnp.pad(w, ((0, n - w.shape[0]), (0, 0)))


def _lora(w_down, w_up):
    rank = -(-w_down.shape[1] // LANES) * LANES
    return _bf(_pad_cols(w_down, rank)), _bf(_pad_rows(w_up, rank))


def _rwkv_layer(h, batch, seq, gn, mu, w_in, w0, w1, w2, a0, a1, a2, g1, g2, k_k, k_a, r_k, ln_w, ln_b,
                v_first, vres):
    vec = lambda x: x.reshape(1, D_MODEL)
    w1p, w2p = _lora(w1, w2)
    a1p, a2p = _lora(a1, a2)
    g1p, g2p = _lora(g1, g2)
    if vres is not None:
        v0, v1, v2 = vres
        v1p, v2p = _lora(v1, v2)
        vres = (v_first, vec(v0), v1p, v2p)
    mu8 = jnp.pad(mu, ((0, 2), (0, 0)))
    r, lw, k, v, an, bb, g = _rwkv_prep(h, seq, vec(gn), mu8, _bf(w_in), vec(w0), w1p, w2p, vec(a0), a1p, a2p,
                                        g1p, g2p, vec(k_k), vec(k_a), vres)
    y = _wkv(batch, seq, r, lw, k, v, an, bb, g, vec(r_k), vec(ln_w), vec(ln_b))
    return y, v


def _nsa_layer(h, batch, seq, gn, w_in, cmp_pe, cmp_w1, cmp_w2, q_norm, k_norm):
    hkv, dh = NSA_HKV, NSA_DH
    nq = D_MODEL
    n_heads = hkv * NSA_G
    kv_cols = hkv * dh
    one = jnp.ones((dh,), F32)
    wq = w_in[:, :nq]
    qg = jnp.tile(q_norm * Q_PRESCALE, n_heads).reshape(1, D_MODEL)
    slopes = jnp.exp2(-8.0 * (jnp.arange(n_heads, dtype=F32) + 1.0) / n_heads) * LOG2_E
    s1 = slopes.astype(BF16).astype(F32)
    s2 = (slopes - s1).astype(BF16).astype(F32)
    s3 = (slopes - s1 - s2).astype(BF16).astype(F32)
    terms = jnp.stack([s1, s2, s3], axis=1)
    qc = jnp.concatenate([jnp.zeros((n_heads, dh), F32), terms, terms * ATT_TK,
                          jnp.zeros((n_heads, LANES - dh - 2 * N_POS_LANES), F32)], axis=1).reshape(1, QA_COLS)

    wkv = w_in[:, nq:nq + 6 * kv_cols].reshape(D_MODEL, 6, hkv, dh)
    wkva = jnp.stack([wkv[:, 2], wkv[:, 4], wkv[:, 3], wkv[:, 5]], axis=2).reshape(D_MODEL, 4 * kv_cols)
    wkvc = jnp.stack([wkv[:, 0], wkv[:, 1]], axis=2).reshape(D_MODEL, KVC_COLS)
    wg = _pad_cols(w_in[:, nq + 6 * kv_cols:], LANES)
    kg = jnp.tile(jnp.concatenate([k_norm[1], k_norm[2]]), hkv).reshape(1, hkv * LANES)
    q, kva, kvc, gt = _nsa_proj(h, seq, gn.reshape(1, D_MODEL), _bf(wq), _bf(wkva), _bf(wkvc), _bf(wg), qg, qc, kg)

    w1 = cmp_w1.reshape(2, 2 * CMP_STRIDE, dh, CMP_HID)
    zeros = jnp.zeros((2 * CMP_STRIDE, dh, CMP_HID), F32)
    wfull = jnp.concatenate([jnp.concatenate([w1[0], zeros], axis=2),
                             jnp.concatenate([zeros, w1[1]], axis=2)], axis=1)
    pe = jnp.concatenate([cmp_pe[0], cmp_pe[1]], axis=1)
    w2 = jnp.zeros((2 * CMP_HID, 2 * LANES), F32)
    w2 = w2.at[:CMP_HID, :dh].set(cmp_w2[0]).at[CMP_HID:, LANES:LANES + dh].set(cmp_w2[1])
    kg0 = jnp.concatenate([k_norm[0], one, one, one]).reshape(1, 2 * LANES)
    cmp = _nsa_cmp(batch, seq, kvc, _bf(wfull[:CMP_STRIDE]), _bf(wfull[CMP_STRIDE:]),
                   pe[:CMP_STRIDE], pe[CMP_STRIDE:], _bf(w2), kg0)

    key = jnp.arange(seq + ATT_TK)[:, None]
    lanes = jnp.arange(LANES)[None, :]
    onehot = jnp.where(key < seq, lanes == key // SLC_BLOCK, lanes == PAD_TILE_LANE)
    return _nsa_attn(batch, seq, q, kva, _bf(jnp.where(onehot, MASK_NEG, 0.0)), cmp, gt)


def kernel(x, mix_norm, mlp_norm, mlp_w1, mlp_w2, rwkv_mu, rwkv_w_in, rwkv_w0, rwkv_w1, rwkv_w2, rwkv_a0, rwkv_a1, rwkv_a2, rwkv_v0, rwkv_v1, rwkv_v2, rwkv_g1, rwkv_g2, rwkv_k_k, rwkv_k_a, rwkv_r_k, rwkv_ln_w, rwkv_ln_b, rwkv_w_out, nsa_w_in, nsa_cmp_pe, nsa_cmp_w1, nsa_cmp_w2, nsa_q_norm, nsa_k_norm, nsa_w_out):
    batch, seq, d = x.shape
    depth = mix_norm.shape[0]
    h = x.reshape(batch * seq, d)
    v_first = None
    for i in range(depth):
        j = i // 2
        if i % 2 == 0:
            vres = None if j == 0 else (rwkv_v0[j - 1], rwkv_v1[j - 1], rwkv_v2[j - 1])
            a, v_raw = _rwkv_layer(h, batch, seq, mix_norm[i], rwkv_mu[j], rwkv_w_in[j], rwkv_w0[j], rwkv_w1[j],
                                   rwkv_w2[j], rwkv_a0[j], rwkv_a1[j], rwkv_a2[j], rwkv_g1[j], rwkv_g2[j],
                                   rwkv_k_k[j], rwkv_k_a[j], rwkv_r_k[j].reshape(-1), rwkv_ln_w[j], rwkv_ln_b[j],
                                   v_first, vres)
            if j == 0:
                v_first = v_raw
            wo = rwkv_w_out[j]
        else:
            a = _nsa_layer(h, batch, seq, mix_norm[i], nsa_w_in[j], nsa_cmp_pe[j], nsa_cmp_w1[j], nsa_cmp_w2[j],
                           nsa_q_norm[j], nsa_k_norm[j])
            wo = nsa_w_out[j]
        h = _out_mlp(h, a, _bf(wo), mlp_norm[i].reshape(1, d), _bf(mlp_w1[i]), _bf(mlp_w2[i]))
    return h.reshape(batch, seq, d)
```

```python
import functools

import jax
import jax.numpy as jnp
import numpy as np
from jax import lax
from jax.experimental import pallas as pl
from jax.experimental.pallas import tpu as pltpu

F32 = jnp.float32
BF16 = jnp.bfloat16

D_MODEL = 1024
MLP_HIDDEN = 4 * D_MODEL
NORM_EPS = 1e-6
LANES = 128
VMEM_LIMIT = 56 * 1024 * 1024

RWKV_N = 64
RWKV_GN_EPS = 64e-5
WKV_CHUNK = 64
WKV_TBLOCK = 256
WKV_PAIRS = 8

NSA_HKV = 4
NSA_G = 4
NSA_DH = 64
CMP_STRIDE = 16
CMP_HID = 128
SLC_BLOCK = 64
N_SELECT = 16
WINDOW = 512
SEL_BIG = 1e9
MASK_NEG = -1e30
LOG2_E = 1.4426950408889634
Q_PRESCALE = NSA_DH ** -0.5 * LOG2_E
PAD_TILE_LANE = LANES - 1


def _dot(a, b):
    return jnp.dot(a, b, preferred_element_type=F32)


def _dot_nt(a, b):
    return lax.dot_general(a, b, (((1,), (1,)), ((), ())), preferred_element_type=F32)


def _bf(x):
    return x.astype(BF16)


def _split2(x):
    hi = x.astype(BF16)
    lo = (x - hi.astype(F32)).astype(BF16)
    return hi, lo


def _dot_hl(x, w):
    hi, lo = _split2(x)
    return _dot(hi, w) + _dot(lo, w)


def _dot_hl_rev(w, x):
    hi, lo = _split2(x)
    return _dot(w, hi) + _dot(w, lo)


def _iota(shape, dim):
    return lax.broadcasted_iota(jnp.int32, shape, dim)


def _bd64(n=LANES):
    return jnp.where((_iota((n, n), 0) >> 6) == (_iota((n, n), 1) >> 6), 1.0, 0.0).astype(BF16)


def _seg64_sum(x, bd, keep=lambda c: True):
    outs = [_dot(_bf(x[:, c * LANES:(c + 1) * LANES]), bd) if keep(c) else
            jnp.zeros((x.shape[0], LANES), F32) for c in range(x.shape[1] // LANES)]
    return outs[0] if len(outs) == 1 else jnp.concatenate(outs, axis=1)


def _rms(x, g):
    return x * lax.rsqrt(jnp.mean(x * x, axis=-1, keepdims=True) + NORM_EPS) * g


def _const_spec(shape):
    return pl.BlockSpec(shape, lambda *_: (0,) * len(shape))


def _params(sem):
    return pltpu.CompilerParams(dimension_semantics=sem, vmem_limit_bytes=VMEM_LIMIT)


MLP_TM = 512
MLP_TH = 1024


def _out_mlp_kernel(h_ref, a_ref, wo_ref, g_ref, w1_ref, w2_ref, o_ref):
    h1 = h_ref[...] + _dot(a_ref[...], wo_ref[...])
    xb = _bf(_rms(h1, g_ref[...]))
    acc = h1
    for c in range(MLP_HIDDEN // MLP_TH):
        u = jnp.maximum(_dot(xb, w1_ref[:, c * MLP_TH:(c + 1) * MLP_TH]), 0.0)
        acc = acc + _dot(_bf(u * u), w2_ref[c * MLP_TH:(c + 1) * MLP_TH, :])
    o_ref[...] = acc


def _out_mlp(h, a, wo, g, w1, w2):
    m = h.shape[0]
    row = lambda i: (i, 0)
    return pl.pallas_call(
        _out_mlp_kernel,
        grid=(m // MLP_TM,),
        in_specs=[pl.BlockSpec((MLP_TM, D_MODEL), row), pl.BlockSpec((MLP_TM, D_MODEL), row),
                  _const_spec((D_MODEL, D_MODEL)), _const_spec((1, D_MODEL)),
                  _const_spec((D_MODEL, MLP_HIDDEN)), _const_spec((MLP_HIDDEN, D_MODEL))],
        out_specs=pl.BlockSpec((MLP_TM, D_MODEL), row),
        out_shape=jax.ShapeDtypeStruct((m, D_MODEL), F32),
        compiler_params=_params(("parallel",)),
    )(h, a, wo, g, w1, w2)


RW_TM = 256
SHIFT_ROWS = 8


def _rwkv_prep_kernel(*refs, tiles_per_seq, vres):
    if vres:
        (h_ref, hp_ref, gn_ref, mu_ref, win_ref, w0_ref, w1_ref, w2_ref, a0_ref, a1_ref, a2_ref,
         g1_ref, g2_ref, kk_ref, ka_ref, vf_ref, v0_ref, v1_ref, v2_ref,
         r_o, lw_o, k_o, v_o, an_o, bb_o, g_o) = refs
    else:
        (h_ref, hp_ref, gn_ref, mu_ref, win_ref, w0_ref, w1_ref, w2_ref, a0_ref, a1_ref, a2_ref,
         g1_ref, g2_ref, kk_ref, ka_ref,
         r_o, lw_o, k_o, v_o, an_o, bb_o, g_o) = refs
    gn = gn_ref[...]
    xn = _rms(h_ref[...], gn)
    first = (pl.program_id(0) % tiles_per_seq) == 0
    prev = _rms(hp_ref[SHIFT_ROWS - 1:SHIFT_ROWS, :], gn)
    prev = jnp.where(first, 0.0, prev)
    xs = pltpu.roll(xn, 1, axis=0)
    xs = jnp.where(_iota(xn.shape, 0) == 0, prev, xs)
    dx = xs - xn
    xr, xw, xk, xv, xa, xg = (xn + dx * mu_ref[i:i + 1, :] for i in range(6))
    r = _dot(_bf(xr), win_ref[:, 0:D_MODEL])
    k = _dot(_bf(xk), win_ref[:, D_MODEL:2 * D_MODEL])
    v = _dot(_bf(xv), win_ref[:, 2 * D_MODEL:3 * D_MODEL])
    if vres:
        lo = _dot(_bf(_dot(_bf(xv), v1_ref[...])), v2_ref[...])
        v = v + (vf_ref[...] - v) * jax.nn.sigmoid(v0_ref[...] + lo)
    z = w0_ref[...] + _dot(_bf(jnp.tanh(_dot(_bf(xw), w1_ref[...]))), w2_ref[...])
    softplus = jnp.maximum(-z, 0.0) + jnp.log(1.0 + jnp.exp(-jnp.abs(z)))
    lw = -jnp.exp(-softplus - 0.5)
    a = jax.nn.sigmoid(a0_ref[...] + _dot(_bf(_dot(_bf(xa), a1_ref[...])), a2_ref[...]))
    g = _dot(_bf(jax.nn.sigmoid(_dot(_bf(xg), g1_ref[...]))), g2_ref[...])
    kk = k * kk_ref[...]
    nrm = jnp.sqrt(_seg64_sum(kk * kk, _bd64()))
    kk = kk / jnp.maximum(nrm, 1e-12)
    ii = _iota((RW_TM, RW_TM), 0)
    jj = _iota((RW_TM, RW_TM), 1)
    sh = WKV_CHUNK.bit_length() - 1
    cum_mat = jnp.where(((ii >> sh) == (jj >> sh)) & (jj <= ii), 1.0, 0.0).astype(BF16)
    r_o[...] = r
    lw_o[...] = _dot_hl_rev(cum_mat, lw)
    k_o[...] = k * (1.0 + (a - 1.0) * ka_ref[...])
    v_o[...] = v
    an_o[...] = -kk
    bb_o[...] = kk * a
    g_o[...] = g


def _rwkv_prep(h, seq, gn, mu, win, w0, w1, w2, a0, a1, a2, g1, g2, k_k, k_a, vres):
    m = h.shape[0]
    row = lambda i: (i, 0)
    prev = lambda i: (jnp.maximum(i * (RW_TM // SHIFT_ROWS) - 1, 0), 0)
    vec = _const_spec((1, D_MODEL))
    full = lambda a: _const_spec(a.shape)
    args = [h, h, gn, mu, win, w0, w1, w2, a0, a1, a2, g1, g2, k_k, k_a]
    specs = [pl.BlockSpec((RW_TM, D_MODEL), row), pl.BlockSpec((SHIFT_ROWS, D_MODEL), prev), vec, full(mu),
             full(win), vec, full(w1), full(w2), vec, full(a1), full(a2), full(g1), full(g2), vec, vec]
    if vres is not None:
        v_first, v0, v1, v2 = vres
        args += [v_first, v0, v1, v2]
        specs += [pl.BlockSpec((RW_TM, D_MODEL), row), vec, full(v1), full(v2)]
    out = jax.ShapeDtypeStruct((m, D_MODEL), F32)
    return pl.pallas_call(
        functools.partial(_rwkv_prep_kernel, tiles_per_seq=seq // RW_TM, vres=vres is not None),
        grid=(m // RW_TM,),
        in_specs=specs,
        out_specs=[pl.BlockSpec((RW_TM, D_MODEL), row)] * 7,
        out_shape=[out] * 7,
        compiler_params=_params(("parallel",)),
    )(*args)


def _wkv_kernel(r_ref, cum_ref, k_ref, v_ref, a_ref, b_ref, g_ref, rk_ref, lnw_ref, lnb_ref,
                o_ref, s_ref, y_ref):
    L = WKV_CHUNK
    L2 = 2 * L

    @pl.when(pl.program_id(2) == 0)
    def _():
        s_ref[...] = jnp.zeros_like(s_ref)

    head0 = _iota((1, LANES), 1) < RWKV_N
    ii = _iota((L2, L2), 0)
    jj = _iota((L2, L2), 1)
    same = (ii >= L) == (jj >= L)
    tril_strict = same & (jj < ii)
    tril_incl = same & (jj <= ii)
    eye = jnp.where(ii == jj, 1.0, 0.0)
    first_row = _iota((L, LANES), 0) == 0
    bd = (_iota((LANES, LANES), 0) >> 6) == (_iota((LANES, LANES), 1) >> 6)

    def by_head(x):
        return jnp.concatenate([jnp.where(head0, x, 0.0), jnp.where(head0, 0.0, x)], axis=0)

    def twice(x):
        return jnp.concatenate([x, x], axis=0)

    def pick(x2):
        return jnp.where(head0, x2[:L], x2[L:])

    def chunk(sl, cols, s):
        cum = cum_ref[sl, cols]
        cum_ex = jnp.where(first_row, 0.0, pltpu.roll(cum, 1, axis=0))
        g_in = jnp.exp(cum)
        g_inv = jnp.exp(-cum)
        cum_l = cum[L - 1:L, :]
        g_to_end = jnp.exp(cum_l - cum)
        r = r_ref[sl, cols]
        k = k_ref[sl, cols]
        v = v_ref[sl, cols]
        b = b_ref[sl, cols]
        at = a_ref[sl, cols] * jnp.exp(cum_ex)
        rt = r * g_in
        bt = b * g_inv
        kt = k * g_inv
        sc = _dot_nt(_bf(jnp.concatenate([by_head(at), by_head(rt)], axis=0)),
                     _bf(jnp.concatenate([bt, bt, kt, kt], axis=0)))
        yield
        m_ab = jnp.where(tril_strict, sc[0:L2, 0:L2], 0.0)
        m_ak = jnp.where(tril_strict, sc[0:L2, L2:2 * L2], 0.0)
        n_rb = jnp.where(tril_incl, sc[L2:2 * L2, 0:L2], 0.0)
        n_rk = jnp.where(tril_incl, sc[L2:2 * L2, L2:2 * L2], 0.0)
        inv = eye + m_ab
        pw = _dot(_bf(m_ab), _bf(m_ab))
        yield
        for _ in range(int(np.log2(L)) - 2):
            both = _dot(_bf(jnp.concatenate([pw, inv], axis=0)), _bf(pw))
            yield
            pw, inv = both[0:L2], inv + both[L2:2 * L2]
        inv = inv + _dot(_bf(inv), _bf(pw))
        yield
        ps = _dot_nt(_bf(jnp.concatenate([at, rt], axis=0)), _bf(s))
        yield
        mv = _dot(_bf(jnp.concatenate([m_ak, n_rk], axis=0)), _bf(twice(v)))
        yield
        u = pick(_dot(_bf(inv), _bf(twice(ps[:L]) + mv[0:L2])))
        yield
        y_ref[sl, cols] = pick(twice(ps[L:]) + _dot(_bf(n_rb), _bf(twice(u))) + mv[L2:2 * L2])
        yield
        uv = jnp.concatenate([u, v], axis=0)
        bk = jnp.concatenate([b * g_to_end, k * g_to_end], axis=0)
        return s * jnp.exp(cum_l) + jnp.where(bd, _dot(_bf(uv.T), _bf(bk)), 0.0)

    ones = jnp.where(bd, 1.0, 0.0).astype(BF16)

    def pair(p):
        cols = slice(p * LANES, (p + 1) * LANES)
        s = s_ref[p]
        for c in range(WKV_TBLOCK // L):
            s = yield from chunk(pl.ds(c * L, L), cols, s)
            yield
        s_ref[p] = s
        y = y_ref[:, cols]
        mean = _dot_hl(y, ones) * (1.0 / RWKV_N)
        yield
        d = y - mean
        var = _dot(_bf(d * d), ones) * (1.0 / RWKV_N)
        yield
        yn = d * lax.rsqrt(var + RWKV_GN_EPS) * lnw_ref[:, cols] + lnb_ref[:, cols]
        bonus = _dot_hl(r_ref[:, cols] * k_ref[:, cols] * rk_ref[:, cols], ones) * v_ref[:, cols]
        o_ref[:, cols] = _bf((yn + bonus) * g_ref[:, cols])

    running = [pair(p) for p in range(WKV_PAIRS)]
    done = object()
    while running:
        running = [gen for gen in running if next(gen, done) is not done]


def _wkv(batch, seq, r, cum, k, v, an, bb, g, r_k, ln_w, ln_b):
    m = r.shape[0]
    nt = seq // WKV_TBLOCK
    width = WKV_PAIRS * LANES
    blk = pl.BlockSpec((WKV_TBLOCK, width), lambda b, p, i: (b * nt + i, p))
    vec = pl.BlockSpec((1, width), lambda b, p, i: (0, p))
    return pl.pallas_call(
        _wkv_kernel,
        grid=(batch, D_MODEL // width, nt),
        in_specs=[blk] * 7 + [vec] * 3,
        out_specs=blk,
        out_shape=jax.ShapeDtypeStruct((m, D_MODEL), BF16),
        scratch_shapes=[pltpu.VMEM((WKV_PAIRS, LANES, LANES), F32), pltpu.VMEM((WKV_TBLOCK, width), F32)],
        compiler_params=_params(("parallel", "parallel", "arbitrary")),
    )(r, cum, k, v, an, bb, g, r_k, ln_w, ln_b)


ATT_TQ = 256
ATT_TK = 256
NSA_TM = ATT_TK
N_POS_LANES = 3
KVA_HEAD_COLS = 4 * LANES
KVA_COLS = NSA_HKV * KVA_HEAD_COLS
KVC_COLS = NSA_HKV * LANES
QA_COLS = NSA_HKV * NSA_G * LANES
GATE_COLS = NSA_HKV * LANES


def _nsa_proj_kernel(h_ref, gn_ref, wq_ref, wkv_ref, wkvc_ref, wg_ref, qg_ref, qc_ref, kg_ref,
                     q_o, kva_o, kvc_o, gt_o, *, tiles_per_seq):
    xb = _bf(_rms(h_ref[...], gn_ref[...]))
    bd = _bd64()
    inv_dh = 1.0 / NSA_DH
    lane = _iota((NSA_TM, LANES), 1)
    low = lane < NSA_DH

    def spread(pair, fill_a, fill_b):
        return [jnp.where(low, pair, fill_a), jnp.where(low, pltpu.roll(pair, NSA_DH, axis=1), fill_b)]

    q = _dot(xb, wq_ref[...])
    q = q * lax.rsqrt(_seg64_sum(q * q, bd) * inv_dh + NORM_EPS) * qg_ref[...]
    outs = []
    for c in range(D_MODEL // LANES):
        outs += spread(q[:, c * LANES:(c + 1) * LANES], qc_ref[:, 2 * c * LANES:(2 * c + 1) * LANES],
                       qc_ref[:, (2 * c + 1) * LANES:(2 * c + 2) * LANES])
    q_o[...] = _bf(jnp.concatenate(outs, axis=1))

    pos_mod = _iota((NSA_TM, LANES), 0).astype(F32)
    pos_tile = (pl.program_id(0) % tiles_per_seq).astype(F32)
    key_fill = jnp.where(lane < NSA_DH + N_POS_LANES, pos_mod, jnp.where(lane < NSA_DH + 2 * N_POS_LANES, pos_tile, 0.0))
    val_fill = jnp.where(lane == NSA_DH, 1.0, 0.0)
    kv = _dot(xb, wkv_ref[...])
    outs = []
    for hd in range(NSA_HKV):
        keys = kv[:, 2 * hd * LANES:(2 * hd + 1) * LANES]
        keys = keys * lax.rsqrt(_dot(_bf(keys * keys), bd) * inv_dh + NORM_EPS) * kg_ref[:, hd * LANES:(hd + 1) * LANES]
        outs += spread(keys, key_fill, key_fill)
        outs += spread(kv[:, (2 * hd + 1) * LANES:(2 * hd + 2) * LANES], val_fill, val_fill)
    kva_o[...] = _bf(jnp.concatenate(outs, axis=1))

    kvc_o[...] = _dot(xb, wkvc_ref[...])
    gates = jax.nn.sigmoid(_dot(xb, wg_ref[...]))
    per_head = 3 * NSA_G
    gt_o[...] = jnp.concatenate([gates if hd == 0 else pltpu.roll(gates, LANES - per_head * hd, axis=1)
                                 for hd in range(NSA_HKV)], axis=1)


def _nsa_proj(h, seq, gn, wq, wkv, wkvc, wg, qg, qc, kg):
    m = h.shape[0]
    row = lambda i: (i, 0)
    return pl.pallas_call(
        functools.partial(_nsa_proj_kernel, tiles_per_seq=seq // NSA_TM),
        grid=(m // NSA_TM,),
        in_specs=[pl.BlockSpec((NSA_TM, D_MODEL), row), _const_spec((1, D_MODEL)),
                  _const_spec(wq.shape), _const_spec(wkv.shape), _const_spec(wkvc.shape), _const_spec(wg.shape),
                  _const_spec(qg.shape), _const_spec(qc.shape), _const_spec(kg.shape)],
        out_specs=[pl.BlockSpec((NSA_TM, QA_COLS), row), pl.BlockSpec((NSA_TM, KVA_COLS), row),
                   pl.BlockSpec((NSA_TM, KVC_COLS), row), pl.BlockSpec((NSA_TM, GATE_COLS), row)],
        out_shape=[jax.ShapeDtypeStruct((m, QA_COLS), BF16), jax.ShapeDtypeStruct((m, KVA_COLS), BF16),
                   jax.ShapeDtypeStruct((m, KVC_COLS), F32), jax.ShapeDtypeStruct((m, GATE_COLS), F32)],
        compiler_params=_params(("parallel",)),
    )(h, gn, wq, wkv, wkvc, wg, qg, qc, kg)


def _nsa_cmp_kernel(kv_ref, wa_ref, wb_ref, pa_ref, pb_ref, w2_ref, kg_ref, o_ref, *, n_blk):
    nb = n_blk + 1
    first = jnp.zeros((nb, 2 * CMP_HID), F32)
    second = jnp.zeros((nb, 2 * CMP_HID), F32)
    for p in range(CMP_STRIDE):
        x = kv_ref[pl.ds(p, nb, stride=CMP_STRIDE), :]
        first = first + _dot(_bf(x + pa_ref[p:p + 1, :]), wa_ref[p])
        second = second + _dot(_bf(x + pb_ref[p:p + 1, :]), wb_ref[p])
    hid = jax.nn.gelu(first + pltpu.roll(second, nb - 1, axis=0))
    out = _dot(_bf(hid), w2_ref[...])
    ss = _seg64_sum(out * out, _bd64()) * (1.0 / NSA_DH)
    lane = _iota((nb, 2 * LANES), 1)
    out = jnp.where(lane < NSA_DH, out * lax.rsqrt(ss + NORM_EPS) * kg_ref[...], out)
    end = _iota((nb, 2 * LANES), 0) * CMP_STRIDE + (2 * CMP_STRIDE - 1)
    sh = ATT_TK.bit_length() - 1
    out = jnp.where((lane >= NSA_DH) & (lane < NSA_DH + N_POS_LANES), (end & (ATT_TK - 1)).astype(F32), out)
    out = jnp.where((lane >= NSA_DH + N_POS_LANES) & (lane < NSA_DH + 2 * N_POS_LANES), (end >> sh).astype(F32), out)
    o_ref[...] = _bf(out)


def _nsa_cmp(batch, seq, kv, wa, wb, pa, pb, w2, kg):
    n_blk = seq // CMP_STRIDE - 1
    nb = n_blk + 1
    return pl.pallas_call(
        functools.partial(_nsa_cmp_kernel, n_blk=n_blk),
        grid=(batch, NSA_HKV),
        in_specs=[pl.BlockSpec((seq, LANES), lambda b, h: (b, h)),
                  _const_spec(wa.shape), _const_spec(wb.shape), _const_spec(pa.shape), _const_spec(pb.shape),
                  _const_spec(w2.shape), _const_spec(kg.shape)],
        out_specs=pl.BlockSpec((None, None, nb, 2 * LANES), lambda b, h: (b, h, 0, 0)),
        out_shape=jax.ShapeDtypeStruct((batch, NSA_HKV, nb, 2 * LANES), BF16),
        compiler_params=_params(("parallel", "parallel")),
    )(kv, wa, wb, pa, pb, w2, kg)


def _nsa_attn_kernel(q_ref, kv_ref, oh_ref, cmp_ref, gt_ref, o_ref,
                     qam_s, m_s, acc_s, sa_s, sb_s, ow_s, tiles_s, *, n_blk, n_sel, n_slc):
    tq, tk = ATT_TQ, ATT_TK
    R = NSA_G * tq
    qi = pl.program_id(2)
    q0 = qi * tq
    t_one = q0 + _iota((tq, 1), 0)
    gt = gt_ref[...]
    n_ch = NSA_G
    qa = jnp.concatenate([q_ref[:, g * LANES:(g + 1) * LANES] for g in range(NSA_G)], axis=0)
    qam_s[:, 0:LANES] = qa

    n_win = WINDOW // tk + 1
    wrows = pl.ds(pl.multiple_of(jnp.maximum(qi - (n_win - 1), 0) * tk, tk), n_win * tk)

    def window():
        dist = t_one - (wrows.start + _iota((1, n_win * tk), 1))
        in_window = dist.astype(jnp.uint32) < WINDOW
        s_all = _dot_nt(qa, kv_ref[wrows, LANES:2 * LANES])
        yield
        pes = []
        for c in range(n_ch):
            s = jnp.where(in_window, s_all[c * tq:(c + 1) * tq], MASK_NEG)
            pes.append(_bf(jnp.exp2(s - jnp.max(s, axis=-1, keepdims=True))))
            yield
        pv = _dot(jnp.concatenate(pes, axis=0), kv_ref[wrows, 3 * LANES:4 * LANES]).reshape(NSA_G, tq, LANES)
        yield
        for g in range(NSA_G):
            ow_s[g] = (gt[:, 3 * g + 2:3 * g + 3] / pv[g][:, NSA_DH:NSA_DH + 1]) * pv[g][:, 0:NSA_DH]

    picked = {}

    def compressed():
        nb = cmp_ref.shape[0]
        n_idx = _iota((1, nb), 1)
        ps = []
        valid = (n_idx * CMP_STRIDE + (2 * CMP_STRIDE - 1) <= t_one) & (n_idx < n_blk)
        qk = _dot_nt(qa, cmp_ref[:, 0:LANES])
        yield
        for g in range(NSA_G):
            s = jnp.where(valid, qk[g * tq:(g + 1) * tq], -jnp.inf)
            mx = jnp.max(s, axis=-1, keepdims=True)
            mx = jnp.where(mx == -jnp.inf, 0.0, mx)
            e = jnp.exp2(s - mx)
            ps.append(e * (1.0 / jnp.maximum(jnp.sum(e, axis=-1, keepdims=True), 1e-30)))
            yield
        picked["o_cmp"] = _dot(_bf(jnp.concatenate(ps, axis=0)), cmp_ref[:, LANES:LANES + NSA_DH])
        yield
        psum = ps[0] + ps[1] + ps[2] + ps[3]
        sj = _iota((n_slc, nb), 0) * SLC_BLOCK
        cn = _iota((n_slc, nb), 1) * CMP_STRIDE
        overlap_t = jnp.where((cn <= sj + SLC_BLOCK - 1) & (cn + 2 * CMP_STRIDE - 1 >= sj), 1.0, 0.0).astype(BF16)
        p_hi, p_lo = _split2(psum)
        imp = _dot_nt(overlap_t, p_hi) + _dot_nt(overlap_t, p_lo)
        yield
        blk = _iota((n_slc, 1), 0)
        ahead_of_cur = ((q0 + _iota((1, tq), 1)) >> 6) - blk
        forced = (blk * ahead_of_cur * (ahead_of_cur - 1)) == 0
        imp = jnp.where(forced, SEL_BIG, imp)
        imp = jnp.where(ahead_of_cur >= 0, imp, -SEL_BIG)
        sub = 8
        ranks = []
        for lo in range(0, n_slc, sub):
            mine = imp[lo:lo + sub, :]
            rk = jnp.zeros((sub, tq), F32)
            for j in range(n_slc):
                other = imp[j:j + 1, :]
                if j < lo:
                    rk = rk + jnp.where(other >= mine, 1.0, 0.0)
                elif j >= lo + sub:
                    rk = rk + jnp.where(other > mine, 1.0, 0.0)
                else:
                    rk = rk + jnp.where(blk[lo:lo + sub] > j, jnp.where(other >= mine, 1.0, 0.0),
                                        jnp.where(other > mine, 1.0, 0.0))
            ranks.append(rk)
            yield
        rank = jnp.concatenate(ranks, axis=0)
        picked["not_sel_t"] = jnp.where(rank < n_sel, 0.0, 1.0)

    branches = {"w": window(), "c": compressed()}
    for name in "c" + "w" + "cccc" + "cc" + "wwww" + "w" + "c" * (n_slc // 8) + "cw":
        next(branches[name], None)
    assert all(next(branch, "finished") == "finished" for branch in branches.values())
    o_cmp, not_sel_t = picked["o_cmp"], picked["not_sel_t"]
    qm = jnp.concatenate([not_sel_t.T, jnp.zeros((tq, LANES - n_slc), F32)], axis=1)
    qm = _bf(jnp.where(_iota((tq, LANES), 1) == PAD_TILE_LANE, 1.0, qm))
    for g in range(NSA_G):
        qam_s[g * tq:(g + 1) * tq, LANES:2 * LANES] = qm

    n_kt = n_slc * SLC_BLOCK // tk
    in_tile = (_iota((n_kt, n_slc), 1) * SLC_BLOCK // tk) == _iota((n_kt, n_slc), 0)
    picks = _dot(jnp.where(in_tile, 1.0, 0.0).astype(BF16), _bf(1.0 - not_sel_t))
    needed = jnp.max(picks, axis=1, keepdims=True)
    n_int = jnp.int32(0)
    for kt in range(n_kt):
        tiles_s[n_int] = kt
        n_int = n_int + ((needed[kt, 0] > 0.0) & (kt < qi)).astype(jnp.int32)
    n_even = n_int + (n_int & 1)
    tiles_s[n_int] = n_kt
    tiles_s[n_even] = qi
    tiles_s[n_even + 1] = qi

    m_s[...] = jnp.full_like(m_s, MASK_NEG)
    acc_s[...] = jnp.zeros_like(acc_s)

    def key_rows(j):
        kt = tiles_s[j]
        return (pl.ds(pl.multiple_of(jnp.minimum(kt, n_kt - 1) * tk, tk), tk),
                pl.ds(pl.multiple_of(kt * tk, tk), tk))

    def scores(j, s_ref):
        kv_rows, oh_rows = key_rows(j)
        kaug = jnp.concatenate([kv_ref[kv_rows, 0:LANES], oh_ref[oh_rows, :]], axis=1)
        s_ref[...] = _dot_nt(qam_s[...], kaug)

    def softmax_pv(j, s_ref, diagonal):
        kv_rows, _ = key_rows(j)
        pes, alphas = [], []
        causal = q0 + _iota((1, tk), 1) <= t_one
        for c in range(n_ch):
            s = s_ref[c * tq:(c + 1) * tq, :]
            if diagonal:
                s = jnp.where(causal, s, MASK_NEG)
            m_old = m_s[c]
            m_new = jnp.maximum(m_old, jnp.max(s, axis=-1, keepdims=True))
            pes.append(_bf(jnp.exp2(s - jnp.concatenate([m_new] * (tk // LANES), axis=1))))
            alphas.append(jnp.exp2(m_old - m_new))
            m_s[c] = m_new
        pv = _dot(jnp.concatenate(pes, axis=0), kv_ref[kv_rows, 2 * LANES:3 * LANES])
        for c in range(n_ch):
            acc_s[c] = alphas[c] * acc_s[c] + pv[c * tq:(c + 1) * tq]

    def sel_body(jj, carry):
        j = 2 * jj
        scores(j + 1, sb_s)
        softmax_pv(j, sa_s, False)
        scores(j + 2, sa_s)
        softmax_pv(j + 1, sb_s, False)
        return carry

    scores(0, sa_s)
    lax.fori_loop(0, n_even // 2, sel_body, 0)
    softmax_pv(n_even, sa_s, True)
    acc = acc_s[...].reshape(NSA_G, tq, LANES)

    outs = []
    for g in range(NSA_G):
        w_sel = gt[:, 3 * g + 1:3 * g + 2] / acc[g][:, NSA_DH:NSA_DH + 1]
        outs.append(gt[:, 3 * g:3 * g + 1] * o_cmp[g * tq:(g + 1) * tq] + w_sel * acc[g][:, 0:NSA_DH]
                    + ow_s[g])
    o_ref[...] = _bf(jnp.concatenate(outs, axis=1))


def _nsa_attn(batch, seq, q, kva, onehot, cmp, gt):
    assert ATT_TQ == ATT_TK
    m = q.shape[0]
    nq = seq // ATT_TQ
    n_blk = seq // CMP_STRIDE - 1
    n_slc = seq // SLC_BLOCK
    n_sel = min(N_SELECT, n_slc)
    R = NSA_G * ATT_TQ
    tile = lambda b, h, i: (b * nq + i, h)
    return pl.pallas_call(
        functools.partial(_nsa_attn_kernel, n_blk=n_blk, n_sel=n_sel, n_slc=n_slc),
        grid=(batch, NSA_HKV, nq),
        in_specs=[pl.BlockSpec((ATT_TQ, NSA_G * LANES), tile),
                  pl.BlockSpec((seq, KVA_HEAD_COLS), lambda b, h, i: (b, h)),
                  pl.BlockSpec((seq + ATT_TK, LANES), lambda b, h, i: (0, 0)),
                  pl.BlockSpec((None, None, n_blk + 1, 2 * LANES), lambda b, h, i: (b, h, 0, 0)),
                  pl.BlockSpec((ATT_TQ, LANES), tile)],
        out_specs=pl.BlockSpec((ATT_TQ, NSA_G * NSA_DH), tile),
        out_shape=jax.ShapeDtypeStruct((m, D_MODEL), BF16),
        scratch_shapes=[pltpu.VMEM((R, 2 * LANES), BF16),
                        pltpu.VMEM((NSA_G, ATT_TQ, LANES), F32),
                        pltpu.VMEM((NSA_G, ATT_TQ, LANES), F32),
                        pltpu.VMEM((R, ATT_TK), F32), pltpu.VMEM((R, ATT_TK), F32),
                        pltpu.VMEM((NSA_G, ATT_TQ, NSA_DH), F32),
                        pltpu.SMEM((seq // ATT_TK + 3,), jnp.int32)],
        compiler_params=_params(("parallel", "parallel", "arbitrary")),
    )(q, kva, onehot, cmp, gt)


def _pad_cols(w, n):
    return jnp.pad(w, ((0, 0), (0, n - w.shape[1])))


def _pad_rows(w, n):
    return jnp.pad(w, ((0, n - w.shape[0]), (0, 0)))


def _lora(w_down, w_up):
    rank = -(-w_down.shape[1] // LANES) * LANES
    return _bf(_pad_cols(w_down, rank)), _bf(_pad_rows(w_up, rank))


def _rwkv_layer(h, batch, seq, gn, mu, w_in, w0, w1, w2, a0, a1, a2, g1, g2, k_k, k_a, r_k, ln_w, ln_b,
                v_first, vres):
    vec = lambda x: x.reshape(1, D_MODEL)
    w1p, w2p = _lora(w1, w2)
    a1p, a2p = _lora(a1, a2)
    g1p, g2p = _lora(g1, g2)
    if vres is not None:
        v0, v1, v2 = vres
        v1p, v2p = _lora(v1, v2)
        vres = (v_first, vec(v0), v1p, v2p)
    mu8 = jnp.pad(mu, ((0, 2), (0, 0)))
    r, lw, k, v, an, bb, g = _rwkv_prep(h, seq, vec(gn), mu8, _bf(w_in), vec(w0), w1p, w2p, vec(a0), a1p, a2p,
                                        g1p, g2p, vec(k_k), vec(k_a), vres)
    y = _wkv(batch, seq, r, lw, k, v, an, bb, g, vec(r_k), vec(ln_w), vec(ln_b))
    return y, v


def _nsa_layer(h, batch, seq, gn, w_in, cmp_pe, cmp_w1, cmp_w2, q_norm, k_norm):
    hkv, dh = NSA_HKV, NSA_DH
    nq = D_MODEL
    n_heads = hkv * NSA_G
    kv_cols = hkv * dh
    one = jnp.ones((dh,), F32)
    wq = w_in[:, :nq]
    qg = jnp.tile(q_norm * Q_PRESCALE, n_heads).reshape(1, D_MODEL)
    slopes = jnp.exp2(-8.0 * (jnp.arange(n_heads, dtype=F32) + 1.0) / n_heads) * LOG2_E
    s1 = slopes.astype(BF16).astype(F32)
    s2 = (slopes - s1).astype(BF16).astype(F32)
    s3 = (slopes - s1 - s2).astype(BF16).astype(F32)
    terms = jnp.stack([s1, s2, s3], axis=1)
    qc = jnp.concatenate([jnp.zeros((n_heads, dh), F32), terms, terms * ATT_TK,
                          jnp.zeros((n_heads, LANES - dh - 2 * N_POS_LANES), F32)], axis=1).reshape(1, QA_COLS)

    wkv = w_in[:, nq:nq + 6 * kv_cols].reshape(D_MODEL, 6, hkv, dh)
    wkva = jnp.stack([wkv[:, 2], wkv[:, 4], wkv[:, 3], wkv[:, 5]], axis=2).reshape(D_MODEL, 4 * kv_cols)
    wkvc = jnp.stack([wkv[:, 0], wkv[:, 1]], axis=2).reshape(D_MODEL, KVC_COLS)
    wg = _pad_cols(w_in[:, nq + 6 * kv_cols:], LANES)
    kg = jnp.tile(jnp.concatenate([k_norm[1], k_norm[2]]), hkv).reshape(1, hkv * LANES)
    q, kva, kvc, gt = _nsa_proj(h, seq, gn.reshape(1, D_MODEL), _bf(wq), _bf(wkva), _bf(wkvc), _bf(wg), qg, qc, kg)

    w1 = cmp_w1.reshape(2, 2 * CMP_STRIDE, dh, CMP_HID)
    zeros = jnp.zeros((2 * CMP_STRIDE, dh, CMP_HID), F32)
    wfull = jnp.concatenate([jnp.concatenate([w1[0], zeros], axis=2),
                             jnp.concatenate([zeros, w1[1]], axis=2)], axis=1)
    pe = jnp.concatenate([cmp_pe[0], cmp_pe[1]], axis=1)
    w2 = jnp.zeros((2 * CMP_HID, 2 * LANES), F32)
    w2 = w2.at[:CMP_HID, :dh].set(cmp_w2[0]).at[CMP_HID:, LANES:LANES + dh].set(cmp_w2[1])
    kg0 = jnp.concatenate([k_norm[0], one, one, one]).reshape(1, 2 * LANES)
    cmp = _nsa_cmp(batch, seq, kvc, _bf(wfull[:CMP_STRIDE]), _bf(wfull[CMP_STRIDE:]),
                   pe[:CMP_STRIDE], pe[CMP_STRIDE:], _bf(w2), kg0)

    key = jnp.arange(seq + ATT_TK)[:, None]
    lanes = jnp.arange(LANES)[None, :]
    onehot = jnp.where(key < seq, lanes == key // SLC_BLOCK, lanes == PAD_TILE_LANE)
    return _nsa_attn(batch, seq, q, kva, _bf(jnp.where(onehot, MASK_NEG, 0.0)), cmp, gt)


def kernel(x, mix_norm, mlp_norm, mlp_w1, mlp_w2, rwkv_mu, rwkv_w_in, rwkv_w0, rwkv_w1, rwkv_w2, rwkv_a0, rwkv_a1, rwkv_a2, rwkv_v0, rwkv_v1, rwkv_v2, rwkv_g1, rwkv_g2, rwkv_k_k, rwkv_k_a, rwkv_r_k, rwkv_ln_w, rwkv_ln_b, rwkv_w_out, nsa_w_in, nsa_cmp_pe, nsa_cmp_w1, nsa_cmp_w2, nsa_q_norm, nsa_k_norm, nsa_w_out):
    batch, seq, d = x.shape
    depth = mix_norm.shape[0]
    h = x.reshape(batch * seq, d)
    v_first = None
    for i in range(depth):
        j = i // 2
        if i % 2 == 0:
            vres = None if j == 0 else (rwkv_v0[j - 1], rwkv_v1[j - 1], rwkv_v2[j - 1])
            a, v_raw = _rwkv_layer(h, batch, seq, mix_norm[i], rwkv_mu[j], rwkv_w_in[j], rwkv_w0[j], rwkv_w1[j],
                                   rwkv_w2[j], rwkv_a0[j], rwkv_a1[j], rwkv_a2[j], rwkv_g1[j], rwkv_g2[j],
                                   rwkv_k_k[j], rwkv_k_a[j], rwkv_r_k[j].reshape(-1), rwkv_ln_w[j], rwkv_ln_b[j],
                                   v_first, vres)
            if j == 0:
                v_first = v_raw
            wo = rwkv_w_out[j]
        else:
            a = _nsa_layer(h, batch, seq, mix_norm[i], nsa_w_in[j], nsa_cmp_pe[j], nsa_cmp_w1[j], nsa_cmp_w2[j],
                           nsa_q_norm[j], nsa_k_norm[j])
            wo = nsa_w_out[j]
        h = _out_mlp(h, a, _bf(wo), mlp_norm[i].reshape(1, d), _bf(mlp_w1[i]), _bf(mlp_w2[i]))
    return h.reshape(batch, seq, d)
```

```python
import functools

import jax
import jax.numpy as jnp
import numpy as np
from jax import lax
from jax.experimental import pallas as pl
from jax.experimental.pallas import tpu as pltpu

F32 = jnp.float32
BF16 = jnp.bfloat16

D_MODEL = 1024
MLP_HIDDEN = 4 * D_MODEL
NORM_EPS = 1e-6
LANES = 128
SUBLANES = 8
HEAD_DIM = 64
HEAD_SHIFT = HEAD_DIM.bit_length() - 1
VMEM_LIMIT = 56 * 1024 * 1024

RWKV_N = 64
RWKV_GN_EPS = 64e-5
WKV_CHUNK = 64
WKV_TBLOCK = 256
WKV_PAIRS = 8

NSA_HKV = 4
NSA_G = 4
NSA_DH = 64
CMP_STRIDE = 16
CMP_HID = 128
SLC_BLOCK = 64
N_SELECT = 16
WINDOW = 512
SEL_BIG = 1e9
MASK_NEG = -1e30
LOG2_E = 1.4426950408889634
Q_PRESCALE = NSA_DH ** -0.5 * LOG2_E
PAD_TILE_LANE = LANES - 1


def _dot(a, b):
    return jnp.dot(a, b, preferred_element_type=F32)


def _dot_nt(a, b):
    return lax.dot_general(a, b, (((1,), (1,)), ((), ())), preferred_element_type=F32)


def _bf(x):
    return x.astype(BF16)


def _split2(x):
    hi = x.astype(BF16)
    lo = (x - hi.astype(F32)).astype(BF16)
    return hi, lo


def _dot_hl(x, w):
    hi, lo = _split2(x)
    return _dot(hi, w) + _dot(lo, w)


def _dot_hl_rev(w, x):
    hi, lo = _split2(x)
    return _dot(w, hi) + _dot(w, lo)


def _iota(shape, dim):
    return lax.broadcasted_iota(jnp.int32, shape, dim)


def _bd64(n=LANES):
    return jnp.where((_iota((n, n), 0) >> HEAD_SHIFT) == (_iota((n, n), 1) >> HEAD_SHIFT), 1.0, 0.0).astype(BF16)


def _seg64_sum(x, bd, keep=lambda c: True):
    outs = [_dot(_bf(x[:, c * LANES:(c + 1) * LANES]), bd) if keep(c) else
            jnp.zeros((x.shape[0], LANES), F32) for c in range(x.shape[1] // LANES)]
    return outs[0] if len(outs) == 1 else jnp.concatenate(outs, axis=1)


def _rms(x, g):
    return x * lax.rsqrt(jnp.mean(x * x, axis=-1, keepdims=True) + NORM_EPS) * g


def _const_spec(shape):
    return pl.BlockSpec(shape, lambda *_: (0,) * len(shape))


def _params(sem):
    return pltpu.CompilerParams(dimension_semantics=sem, vmem_limit_bytes=VMEM_LIMIT)


MLP_TM = 512
MLP_TH = 1024


def _out_mlp_kernel(h_ref, a_ref, wo_ref, g_ref, w1_ref, w2_ref, o_ref):
    h1 = h_ref[...] + _dot(a_ref[...], wo_ref[...])
    xb = _bf(_rms(h1, g_ref[...]))
    acc = h1
    for c in range(MLP_HIDDEN // MLP_TH):
        u = jnp.maximum(_dot(xb, w1_ref[:, c * MLP_TH:(c + 1) * MLP_TH]), 0.0)
        acc = acc + _dot(_bf(u * u), w2_ref[c * MLP_TH:(c + 1) * MLP_TH, :])
    o_ref[...] = acc


def _out_mlp(h, a, wo, g, w1, w2):
    m = h.shape[0]
    row = lambda i: (i, 0)
    return pl.pallas_call(
        _out_mlp_kernel,
        grid=(m // MLP_TM,),
        in_specs=[pl.BlockSpec((MLP_TM, D_MODEL), row), pl.BlockSpec((MLP_TM, D_MODEL), row),
                  _const_spec((D_MODEL, D_MODEL)), _const_spec((1, D_MODEL)),
                  _const_spec((D_MODEL, MLP_HIDDEN)), _const_spec((MLP_HIDDEN, D_MODEL))],
        out_specs=pl.BlockSpec((MLP_TM, D_MODEL), row),
        out_shape=jax.ShapeDtypeStruct((m, D_MODEL), F32),
        compiler_params=_params(("parallel",)),
    )(h, a, wo, g, w1, w2)


RW_TM = 256
SHIFT_ROWS = 8


def _rwkv_prep_kernel(*refs, tiles_per_seq, vres):
    if vres:
        (h_ref, hp_ref, gn_ref, mu_ref, win_ref, w0_ref, w1_ref, w2_ref, a0_ref, a1_ref, a2_ref,
         g1_ref, g2_ref, kk_ref, ka_ref, vf_ref, v0_ref, v1_ref, v2_ref,
         r_o, lw_o, k_o, v_o, an_o, bb_o, g_o) = refs
    else:
        (h_ref, hp_ref, gn_ref, mu_ref, win_ref, w0_ref, w1_ref, w2_ref, a0_ref, a1_ref, a2_ref,
         g1_ref, g2_ref, kk_ref, ka_ref,
         r_o, lw_o, k_o, v_o, an_o, bb_o, g_o) = refs
    gn = gn_ref[...]
    xn = _rms(h_ref[...], gn)
    first = (pl.program_id(0) % tiles_per_seq) == 0
    prev = _rms(hp_ref[SHIFT_ROWS - 1:SHIFT_ROWS, :], gn)
    prev = jnp.where(first, 0.0, prev)
    xs = pltpu.roll(xn, 1, axis=0)
    xs = jnp.where(_iota(xn.shape, 0) == 0, prev, xs)
    dx = xs - xn

    def mixed(i):
        return _bf(xn + dx * mu_ref[i:i + 1, :])

    xa = mixed(4)
    a_low = _dot(xa, a1_ref[...])
    r_o[...] = _dot(mixed(0), win_ref[:, 0:D_MODEL])
    a = jax.nn.sigmoid(a0_ref[...] + _dot(_bf(a_low), a2_ref[...]))
    k = _dot(mixed(2), win_ref[:, D_MODEL:2 * D_MODEL])
    xw = mixed(1)
    w_low = jnp.tanh(_dot(xw, w1_ref[...]))
    kk = k * kk_ref[...]
    nrm = jnp.sqrt(_seg64_sum(kk * kk, _bd64()))
    k_o[...] = k * (1.0 + (a - 1.0) * ka_ref[...])
    z = w0_ref[...] + _dot(_bf(w_low), w2_ref[...])
    kk = kk / jnp.maximum(nrm, 1e-12)
    an_o[...] = -kk
    bb_o[...] = kk * a
    xv = mixed(3)
    v = _dot(xv, win_ref[:, 2 * D_MODEL:3 * D_MODEL])
    softplus = jnp.maximum(-z, 0.0) + jnp.log(1.0 + jnp.exp(-jnp.abs(z)))
    lw = -jnp.exp(-softplus - 0.5)
    if vres:
        lo = _dot(_bf(_dot(xv, v1_ref[...])), v2_ref[...])
        v = v + (vf_ref[...] - v) * jax.nn.sigmoid(v0_ref[...] + lo)
    v_o[...] = v
    g_low = jax.nn.sigmoid(_dot(mixed(5), g1_ref[...]))
    ii = _iota((RW_TM, RW_TM), 0)
    jj = _iota((RW_TM, RW_TM), 1)
    sh = WKV_CHUNK.bit_length() - 1
    cum_mat = jnp.where(((ii >> sh) == (jj >> sh)) & (jj <= ii), 1.0, 0.0).astype(BF16)
    lw_o[...] = _dot_hl_rev(cum_mat, lw)
    g_o[...] = _dot(_bf(g_low), g2_ref[...])


def _rwkv_prep(h, seq, gn, mu, win, w0, w1, w2, a0, a1, a2, g1, g2, k_k, k_a, vres):
    m = h.shape[0]
    row = lambda i: (i, 0)
    prev = lambda i: (jnp.maximum(i * (RW_TM // SHIFT_ROWS) - 1, 0), 0)
    vec = _const_spec((1, D_MODEL))
    full = lambda a: _const_spec(a.shape)
    args = [h, h, gn, mu, win, w0, w1, w2, a0, a1, a2, g1, g2, k_k, k_a]
    specs = [pl.BlockSpec((RW_TM, D_MODEL), row), pl.BlockSpec((SHIFT_ROWS, D_MODEL), prev), vec, full(mu),
             full(win), vec, full(w1), full(w2), vec, full(a1), full(a2), full(g1), full(g2), vec, vec]
    if vres is not None:
        v_first, v0, v1, v2 = vres
        args += [v_first, v0, v1, v2]
        specs += [pl.BlockSpec((RW_TM, D_MODEL), row), vec, full(v1), full(v2)]
    out = jax.ShapeDtypeStruct((m, D_MODEL), F32)
    return pl.pallas_call(
        functools.partial(_rwkv_prep_kernel, tiles_per_seq=seq // RW_TM, vres=vres is not None),
        grid=(m // RW_TM,),
        in_specs=specs,
        out_specs=[pl.BlockSpec((RW_TM, D_MODEL), row)] * 7,
        out_shape=[out] * 7,
        compiler_params=_params(("parallel",)),
    )(*args)


def _wkv_kernel(r_ref, cum_ref, k_ref, v_ref, a_ref, b_ref, g_ref, rk_ref, lnw_ref, lnb_ref,
                o_ref, s_ref, y_ref):
    L = WKV_CHUNK
    L2 = 2 * L

    @pl.when(pl.program_id(2) == 0)
    def _():
        s_ref[...] = jnp.zeros_like(s_ref)

    head0 = _iota((1, LANES), 1) < RWKV_N
    ii = _iota((L2, L2), 0)
    jj = _iota((L2, L2), 1)
    same = (ii >= L) == (jj >= L)
    tril_strict = same & (jj < ii)
    tril_incl = same & (jj <= ii)
    eye = jnp.where(ii == jj, 1.0, 0.0)
    first_row = _iota((L, LANES), 0) == 0
    bd = (_iota((LANES, LANES), 0) >> HEAD_SHIFT) == (_iota((LANES, LANES), 1) >> HEAD_SHIFT)

    def by_head(x):
        return jnp.concatenate([jnp.where(head0, x, 0.0), jnp.where(head0, 0.0, x)], axis=0)

    def twice(x):
        return jnp.concatenate([x, x], axis=0)

    def pick(x2):
        return jnp.where(head0, x2[:L], x2[L:])

    def chunk(sl, cols, s):
        cum = cum_ref[sl, cols]
        cum_ex = jnp.where(first_row, 0.0, pltpu.roll(cum, 1, axis=0))
        g_in = jnp.exp(cum)
        g_inv = jnp.exp(-cum)
        cum_l = cum[L - 1:L, :]
        g_to_end = jnp.exp(cum_l - cum)
        r = r_ref[sl, cols]
        k = k_ref[sl, cols]
        v = v_ref[sl, cols]
        b = b_ref[sl, cols]
        at = a_ref[sl, cols] * jnp.exp(cum_ex)
        rt = r * g_in
        bt = b * g_inv
        kt = k * g_inv
        sc = _dot_nt(_bf(jnp.concatenate([by_head(at), by_head(rt)], axis=0)),
                     _bf(jnp.concatenate([bt, bt, kt, kt], axis=0)))
        yield
        m_ab = jnp.where(tril_strict, sc[0:L2, 0:L2], 0.0)
        m_ak = jnp.where(tril_strict, sc[0:L2, L2:2 * L2], 0.0)
        n_rb = jnp.where(tril_incl, sc[L2:2 * L2, 0:L2], 0.0)
        n_rk = jnp.where(tril_incl, sc[L2:2 * L2, L2:2 * L2], 0.0)
        inv = eye + m_ab
        pw = _dot(_bf(m_ab), _bf(m_ab))
        yield
        for _ in range(int(np.log2(L)) - 2):
            both = _dot(_bf(jnp.concatenate([pw, inv], axis=0)), _bf(pw))
            yield
            pw, inv = both[0:L2], inv + both[L2:2 * L2]
        inv = inv + _dot(_bf(inv), _bf(pw))
        yield
        ps = _dot_nt(_bf(jnp.concatenate([at, rt], axis=0)), _bf(s))
        yield
        mv = _dot(_bf(jnp.concatenate([m_ak, n_rk], axis=0)), _bf(twice(v)))
        yield
        u = pick(_dot(_bf(inv), _bf(twice(ps[:L]) + mv[0:L2])))
        yield
        y_ref[sl, cols] = pick(twice(ps[L:]) + _dot(_bf(n_rb), _bf(twice(u))) + mv[L2:2 * L2])
        yield
        uv = jnp.concatenate([u, v], axis=0)
        bk = jnp.concatenate([b * g_to_end, k * g_to_end], axis=0)
        return s * jnp.exp(cum_l) + jnp.where(bd, _dot(_bf(uv.T), _bf(bk)), 0.0)

    ones = jnp.where(bd, 1.0, 0.0).astype(BF16)

    def pair(p):
        cols = slice(p * LANES, (p + 1) * LANES)
        s = s_ref[p]
        for c in range(WKV_TBLOCK // L):
            s = yield from chunk(pl.ds(c * L, L), cols, s)
            yield
        s_ref[p] = s
        y = y_ref[:, cols]
        mean = _dot_hl(y, ones) * (1.0 / RWKV_N)
        yield
        d = y - mean
        var = _dot(_bf(d * d), ones) * (1.0 / RWKV_N)
        yield
        yn = d * lax.rsqrt(var + RWKV_GN_EPS) * lnw_ref[:, cols] + lnb_ref[:, cols]
        bonus = _dot(_bf(r_ref[:, cols] * k_ref[:, cols] * rk_ref[:, cols]), ones) * v_ref[:, cols]
        o_ref[:, cols] = _bf((yn + bonus) * g_ref[:, cols])

    running = [pair(p) for p in range(WKV_PAIRS)]
    done = object()
    while running:
        running = [gen for gen in running if next(gen, done) is not done]


def _wkv(batch, seq, r, cum, k, v, an, bb, g, r_k, ln_w, ln_b):
    m = r.shape[0]
    nt = seq // WKV_TBLOCK
    width = WKV_PAIRS * LANES
    blk = pl.BlockSpec((WKV_TBLOCK, width), lambda b, p, i: (b * nt + i, p))
    vec = pl.BlockSpec((1, width), lambda b, p, i: (0, p))
    return pl.pallas_call(
        _wkv_kernel,
        grid=(batch, D_MODEL // width, nt),
        in_specs=[blk] * 7 + [vec] * 3,
        out_specs=blk,
        out_shape=jax.ShapeDtypeStruct((m, D_MODEL), BF16),
        scratch_shapes=[pltpu.VMEM((WKV_PAIRS, LANES, LANES), F32), pltpu.VMEM((WKV_TBLOCK, width), F32)],
        compiler_params=_params(("parallel", "parallel", "arbitrary")),
    )(r, cum, k, v, an, bb, g, r_k, ln_w, ln_b)


ATT_TQ = 256
ATT_TK = 256
NSA_TM = ATT_TK
N_POS_LANES = 3
KVA_HEAD_COLS = 4 * LANES
KVA_COLS = NSA_HKV * KVA_HEAD_COLS
KVC_COLS = NSA_HKV * LANES
QA_COLS = NSA_HKV * NSA_G * LANES
GATE_COLS = NSA_HKV * LANES


def _nsa_proj_kernel(h_ref, gn_ref, wq_ref, wkv_ref, wkvc_ref, wg_ref, qg_ref, qc_ref, kg_ref,
                     q_o, kva_o, kvc_o, gt_o, *, tiles_per_seq):
    xb = _bf(_rms(h_ref[...], gn_ref[...]))
    bd = _bd64()
    inv_dh = 1.0 / NSA_DH
    lane = _iota((NSA_TM, LANES), 1)
    low = lane < NSA_DH

    def spread(pair, fill_a, fill_b):
        return [jnp.where(low, pair, fill_a), jnp.where(low, pltpu.roll(pair, NSA_DH, axis=1), fill_b)]

    q = _dot(xb, wq_ref[...])
    q = q * lax.rsqrt(_seg64_sum(q * q, bd) * inv_dh + NORM_EPS) * qg_ref[...]
    outs = []
    for c in range(D_MODEL // LANES):
        outs += spread(q[:, c * LANES:(c + 1) * LANES], qc_ref[:, 2 * c * LANES:(2 * c + 1) * LANES],
                       qc_ref[:, (2 * c + 1) * LANES:(2 * c + 2) * LANES])
    q_o[...] = _bf(jnp.concatenate(outs, axis=1))

    pos_mod = _iota((NSA_TM, LANES), 0).astype(F32)
    pos_tile = (pl.program_id(0) % tiles_per_seq).astype(F32)
    key_fill = jnp.where(lane < NSA_DH + N_POS_LANES, pos_mod, jnp.where(lane < NSA_DH + 2 * N_POS_LANES, pos_tile, 0.0))
    val_fill = jnp.where(lane == NSA_DH, 1.0, 0.0)
    kv = _dot(xb, wkv_ref[...])
    outs = []
    for hd in range(NSA_HKV):
        keys = kv[:, 2 * hd * LANES:(2 * hd + 1) * LANES]
        keys = keys * lax.rsqrt(_dot(_bf(keys * keys), bd) * inv_dh + NORM_EPS) * kg_ref[:, hd * LANES:(hd + 1) * LANES]
        outs += spread(keys, key_fill, key_fill)
        outs += spread(kv[:, (2 * hd + 1) * LANES:(2 * hd + 2) * LANES], val_fill, val_fill)
    kva_o[...] = _bf(jnp.concatenate(outs, axis=1))

    kvc_o[...] = _dot(xb, wkvc_ref[...])
    gates = jax.nn.sigmoid(_dot(xb, wg_ref[...]))
    per_head = 3 * NSA_G
    gt_o[...] = jnp.concatenate([gates if hd == 0 else pltpu.roll(gates, LANES - per_head * hd, axis=1)
                                 for hd in range(NSA_HKV)], axis=1)


def _nsa_proj(h, seq, gn, wq, wkv, wkvc, wg, qg, qc, kg):
    m = h.shape[0]
    row = lambda i: (i, 0)
    return pl.pallas_call(
        functools.partial(_nsa_proj_kernel, tiles_per_seq=seq // NSA_TM),
        grid=(m // NSA_TM,),
        in_specs=[pl.BlockSpec((NSA_TM, D_MODEL), row), _const_spec((1, D_MODEL)),
                  _const_spec(wq.shape), _const_spec(wkv.shape), _const_spec(wkvc.shape), _const_spec(wg.shape),
                  _const_spec(qg.shape), _const_spec(qc.shape), _const_spec(kg.shape)],
        out_specs=[pl.BlockSpec((NSA_TM, QA_COLS), row), pl.BlockSpec((NSA_TM, KVA_COLS), row),
                   pl.BlockSpec((NSA_TM, KVC_COLS), row), pl.BlockSpec((NSA_TM, GATE_COLS), row)],
        out_shape=[jax.ShapeDtypeStruct((m, QA_COLS), BF16), jax.ShapeDtypeStruct((m, KVA_COLS), BF16),
                   jax.ShapeDtypeStruct((m, KVC_COLS), F32), jax.ShapeDtypeStruct((m, GATE_COLS), F32)],
        compiler_params=_params(("parallel",)),
    )(h, gn, wq, wkv, wkvc, wg, qg, qc, kg)


def _nsa_cmp_kernel(kv_ref, wa_ref, wb_ref, pa_ref, pb_ref, w2_ref, kg_ref, o_ref, *, n_blk):
    nb = n_blk + 1
    first = jnp.zeros((nb, 2 * CMP_HID), F32)
    second = jnp.zeros((nb, 2 * CMP_HID), F32)
    for p in range(CMP_STRIDE):
        x = kv_ref[pl.ds(p, nb, stride=CMP_STRIDE), :]
        first = first + _dot(_bf(x + pa_ref[p:p + 1, :]), wa_ref[p])
        second = second + _dot(_bf(x + pb_ref[p:p + 1, :]), wb_ref[p])
    hid = jax.nn.gelu(first + pltpu.roll(second, nb - 1, axis=0))
    out = _dot(_bf(hid), w2_ref[...])
    ss = _seg64_sum(out * out, _bd64()) * (1.0 / NSA_DH)
    lane = _iota((nb, 2 * LANES), 1)
    out = jnp.where(lane < NSA_DH, out * lax.rsqrt(ss + NORM_EPS) * kg_ref[...], out)
    end = _iota((nb, 2 * LANES), 0) * CMP_STRIDE + (2 * CMP_STRIDE - 1)
    sh = ATT_TK.bit_length() - 1
    out = jnp.where((lane >= NSA_DH) & (lane < NSA_DH + N_POS_LANES), (end & (ATT_TK - 1)).astype(F32), out)
    out = jnp.where((lane >= NSA_DH + N_POS_LANES) & (lane < NSA_DH + 2 * N_POS_LANES), (end >> sh).astype(F32), out)
    o_ref[...] = _bf(out)


def _nsa_cmp(batch, seq, kv, wa, wb, pa, pb, w2, kg):
    n_blk = seq // CMP_STRIDE - 1
    nb = n_blk + 1
    return pl.pallas_call(
        functools.partial(_nsa_cmp_kernel, n_blk=n_blk),
        grid=(batch, NSA_HKV),
        in_specs=[pl.BlockSpec((seq, LANES), lambda b, h: (b, h)),
                  _const_spec(wa.shape), _const_spec(wb.shape), _const_spec(pa.shape), _const_spec(pb.shape),
                  _const_spec(w2.shape), _const_spec(kg.shape)],
        out_specs=pl.BlockSpec((None, None, nb, 2 * LANES), lambda b, h: (b, h, 0, 0)),
        out_shape=jax.ShapeDtypeStruct((batch, NSA_HKV, nb, 2 * LANES), BF16),
        compiler_params=_params(("parallel", "parallel")),
    )(kv, wa, wb, pa, pb, w2, kg)


def _nsa_attn_kernel(q_ref, kv_ref, oh_ref, cmp_ref, gt_ref, o_ref,
                     qam_s, m_s, acc_s, sa_s, sb_s, ow_s, tiles_s, *, n_blk, n_sel, n_slc):
    tq, tk = ATT_TQ, ATT_TK
    R = NSA_G * tq
    qi = pl.program_id(2)
    q0 = qi * tq
    t_one = q0 + _iota((tq, 1), 0)
    gt = gt_ref[...]
    n_ch = NSA_G
    qa = jnp.concatenate([q_ref[:, g * LANES:(g + 1) * LANES] for g in range(NSA_G)], axis=0)
    qam_s[:, 0:LANES] = qa

    n_win = WINDOW // tk + 1
    wrows = pl.ds(pl.multiple_of(jnp.maximum(qi - (n_win - 1), 0) * tk, tk), n_win * tk)

    def window():
        dist = t_one - (wrows.start + _iota((1, n_win * tk), 1))
        in_window = dist.astype(jnp.uint32) < WINDOW
        s_all = _dot_nt(qa, kv_ref[wrows, LANES:2 * LANES])
        yield
        pes = []
        for c in range(n_ch):
            s = jnp.where(in_window, s_all[c * tq:(c + 1) * tq], MASK_NEG)
            pes.append(_bf(jnp.exp2(s - jnp.max(s, axis=-1, keepdims=True))))
            yield
        pv = _dot(jnp.concatenate(pes, axis=0), kv_ref[wrows, 3 * LANES:4 * LANES]).reshape(NSA_G, tq, LANES)
        yield
        for g in range(NSA_G):
            ow_s[g] = ((gt[:, 3 * g + 2:3 * g + 3] / pv[g][:, NSA_DH:NSA_DH + 1]) * pv[g][:, 0:NSA_DH]
                       + gt[:, 3 * g:3 * g + 1] * picked["o_cmp"][g * tq:(g + 1) * tq])

    picked = {}

    def compressed():
        nb = cmp_ref.shape[0]
        n_idx = _iota((1, nb), 1)
        ps = []
        valid = (n_idx * CMP_STRIDE + (2 * CMP_STRIDE - 1) <= t_one) & (n_idx < n_blk)
        qk = _dot_nt(qa, cmp_ref[:, 0:LANES])
        yield
        for g in range(NSA_G):
            s = jnp.where(valid, qk[g * tq:(g + 1) * tq], -jnp.inf)
            mx = jnp.max(s, axis=-1, keepdims=True)
            mx = jnp.where(mx == -jnp.inf, 0.0, mx)
            e = jnp.exp2(s - mx)
            ps.append(e * (1.0 / jnp.maximum(jnp.sum(e, axis=-1, keepdims=True), 1e-30)))
            yield
        picked["o_cmp"] = _dot(_bf(jnp.concatenate(ps, axis=0)), cmp_ref[:, LANES:LANES + NSA_DH])
        yield
        psum = ps[0] + ps[1] + ps[2] + ps[3]
        sj = _iota((n_slc, nb), 0) * SLC_BLOCK
        cn = _iota((n_slc, nb), 1) * CMP_STRIDE
        overlap_t = jnp.where((cn <= sj + SLC_BLOCK - 1) & (cn + 2 * CMP_STRIDE - 1 >= sj), 1.0, 0.0).astype(BF16)
        p_hi, p_lo = _split2(psum)
        imp = _dot_nt(overlap_t, p_hi) + _dot_nt(overlap_t, p_lo)
        yield
        blk = _iota((n_slc, 1), 0)
        ahead_of_cur = ((q0 + _iota((1, tq), 1)) >> (SLC_BLOCK.bit_length() - 1)) - blk
        forced = (blk * ahead_of_cur * (ahead_of_cur - 1)) == 0
        imp = jnp.where(forced, SEL_BIG, imp)
        imp = jnp.where(ahead_of_cur >= 0, imp, -SEL_BIG)
        sub = SUBLANES
        ranks = []
        for lo in range(0, n_slc, sub):
            mine = imp[lo:lo + sub, :]
            rk = jnp.zeros((sub, tq), F32)
            for j in range(n_slc):
                other = imp[j:j + 1, :]
                if j < lo:
                    rk = rk + jnp.where(other >= mine, 1.0, 0.0)
                elif j >= lo + sub:
                    rk = rk + jnp.where(other > mine, 1.0, 0.0)
                else:
                    rk = rk + jnp.where(blk[lo:lo + sub] > j, jnp.where(other >= mine, 1.0, 0.0),
                                        jnp.where(other > mine, 1.0, 0.0))
            ranks.append(rk)
            yield
        rank = jnp.concatenate(ranks, axis=0)
        picked["not_sel_t"] = jnp.where(rank < n_sel, 0.0, 1.0)

    branches = {"w": window(), "c": compressed()}
    for name in "c" + "w" + "cccc" + "cc" + "wwww" + "w" + "c" * (n_slc // 8) + "cw":
        next(branches[name], None)
    assert all(next(branch, "finished") == "finished" for branch in branches.values())
    not_sel_t = picked["not_sel_t"]
    qm = jnp.concatenate([not_sel_t.T, jnp.zeros((tq, LANES - n_slc), F32)], axis=1)
    qm = _bf(jnp.where(_iota((tq, LANES), 1) == PAD_TILE_LANE, 1.0, qm))
    for g in range(NSA_G):
        qam_s[g * tq:(g + 1) * tq, LANES:2 * LANES] = qm

    n_kt = n_slc * SLC_BLOCK // tk
    in_tile = (_iota((n_kt, n_slc), 1) * SLC_BLOCK // tk) == _iota((n_kt, n_slc), 0)
    picks = _dot(jnp.where(in_tile, 1.0, 0.0).astype(BF16), _bf(1.0 - not_sel_t))
    needed = jnp.max(picks, axis=1, keepdims=True)
    n_int = jnp.int32(0)
    for kt in range(n_kt):
        tiles_s[n_int] = kt
        n_int = n_int + ((needed[kt, 0] > 0.0) & (kt < qi)).astype(jnp.int32)
    n_even = n_int + (n_int & 1)
    tiles_s[n_int] = n_kt
    tiles_s[n_even] = qi
    tiles_s[n_even + 1] = qi

    m_s[...] = jnp.full_like(m_s, MASK_NEG)
    acc_s[...] = jnp.zeros_like(acc_s)

    def key_rows(j):
        kt = tiles_s[j]
        return (pl.ds(pl.multiple_of(jnp.minimum(kt, n_kt - 1) * tk, tk), tk),
                pl.ds(pl.multiple_of(kt * tk, tk), tk))

    def scores(j, s_ref):
        kv_rows, oh_rows = key_rows(j)
        kaug = jnp.concatenate([kv_ref[kv_rows, 0:LANES], oh_ref[oh_rows, :]], axis=1)
        s_ref[...] = _dot_nt(qam_s[...], kaug)

    def softmax_pv(j, s_ref, diagonal):
        kv_rows, _ = key_rows(j)
        pes, alphas = [], []
        causal = q0 + _iota((1, tk), 1) <= t_one
        for c in range(n_ch):
            s = s_ref[c * tq:(c + 1) * tq, :]
            if diagonal:
                s = jnp.where(causal, s, MASK_NEG)
            m_old = m_s[c]
            m_new = jnp.maximum(m_old, jnp.max(s, axis=-1, keepdims=True))
            pes.append(_bf(jnp.exp2(s - jnp.concatenate([m_new] * (tk // LANES), axis=1))))
            alphas.append(jnp.exp2(m_old - m_new))
            m_s[c] = m_new
        pv = _dot(jnp.concatenate(pes, axis=0), kv_ref[kv_rows, 2 * LANES:3 * LANES])
        for c in range(n_ch):
            acc_s[c] = alphas[c] * acc_s[c] + pv[c * tq:(c + 1) * tq]

    def sel_body(jj, carry):
        j = 2 * jj
        scores(j + 1, sb_s)
        softmax_pv(j, sa_s, False)
        scores(j + 2, sa_s)
        softmax_pv(j + 1, sb_s, False)
        return carry

    scores(0, sa_s)
    lax.fori_loop(0, n_even // 2, sel_body, 0)
    softmax_pv(n_even, sa_s, True)
    acc = acc_s[...].reshape(NSA_G, tq, LANES)

    outs = []
    for g in range(NSA_G):
        w_sel = gt[:, 3 * g + 1:3 * g + 2] / acc[g][:, NSA_DH:NSA_DH + 1]
        outs.append(w_sel * acc[g][:, 0:NSA_DH] + ow_s[g])
    o_ref[...] = _bf(jnp.concatenate(outs, axis=1))


def _nsa_attn(batch, seq, q, kva, onehot, cmp, gt):
    assert ATT_TQ == ATT_TK
    m = q.shape[0]
    nq = seq // ATT_TQ
    n_blk = seq // CMP_STRIDE - 1
    n_slc = seq // SLC_BLOCK
    n_sel = min(N_SELECT, n_slc)
    R = NSA_G * ATT_TQ
    tile = lambda b, h, i: (b * nq + i, h)
    return pl.pallas_call(
        functools.partial(_nsa_attn_kernel, n_blk=n_blk, n_sel=n_sel, n_slc=n_slc),
        grid=(batch, NSA_HKV, nq),
        in_specs=[pl.BlockSpec((ATT_TQ, NSA_G * LANES), tile),
                  pl.BlockSpec((seq, KVA_HEAD_COLS), lambda b, h, i: (b, h)),
                  pl.BlockSpec((seq + ATT_TK, LANES), lambda b, h, i: (0, 0)),
                  pl.BlockSpec((None, None, n_blk + 1, 2 * LANES), lambda b, h, i: (b, h, 0, 0)),
                  pl.BlockSpec((ATT_TQ, LANES), tile)],
        out_specs=pl.BlockSpec((ATT_TQ, NSA_G * NSA_DH), tile),
        out_shape=jax.ShapeDtypeStruct((m, D_MODEL), BF16),
        scratch_shapes=[pltpu.VMEM((R, 2 * LANES), BF16),
                        pltpu.VMEM((NSA_G, ATT_TQ, LANES), F32),
                        pltpu.VMEM((NSA_G, ATT_TQ, LANES), F32),
                        pltpu.VMEM((R, ATT_TK), F32), pltpu.VMEM((R, ATT_TK), F32),
                        pltpu.VMEM((NSA_G, ATT_TQ, NSA_DH), F32),
                        pltpu.SMEM((seq // ATT_TK + 3,), jnp.int32)],
        compiler_params=_params(("parallel", "parallel", "arbitrary")),
    )(q, kva, onehot, cmp, gt)


def _pad_cols(w, n):
    return jnp.pad(w, ((0, 0), (0, n - w.shape[1])))


def _pad_rows(w, n):
    return jnp.pad(w, ((0, n - w.shape[0]), (0, 0)))


def _lora(w_down, w_up):
    rank = -(-w_down.shape[1] // LANES) * LANES
    return _bf(_pad_cols(w_down, rank)), _bf(_pad_rows(w_up, rank))


def _rwkv_layer(h, batch, seq, gn, mu, w_in, w0, w1, w2, a0, a1, a2, g1, g2, k_k, k_a, r_k, ln_w, ln_b,
                v_first, vres):
    vec = lambda x: x.reshape(1, D_MODEL)
    w1p, w2p = _lora(w1, w2)
    a1p, a2p = _lora(a1, a2)
    g1p, g2p = _lora(g1, g2)
    if vres is not None:
        v0, v1, v2 = vres
        v1p, v2p = _lora(v1, v2)
        vres = (v_first, vec(v0), v1p, v2p)
    mu8 = jnp.pad(mu, ((0, 2), (0, 0)))
    r, lw, k, v, an, bb, g = _rwkv_prep(h, seq, vec(gn), mu8, _bf(w_in), vec(w0), w1p, w2p, vec(a0), a1p, a2p,
                                        g1p, g2p, vec(k_k), vec(k_a), vres)
    y = _wkv(batch, seq, r, lw, k, v, an, bb, g, vec(r_k), vec(ln_w), vec(ln_b))
    return y, v


def _nsa_layer(h, batch, seq, gn, w_in, cmp_pe, cmp_w1, cmp_w2, q_norm, k_norm):
    hkv, dh = NSA_HKV, NSA_DH
    nq = D_MODEL
    n_heads = hkv * NSA_G
    kv_cols = hkv * dh
    one = jnp.ones((dh,), F32)
    wq = w_in[:, :nq]
    qg = jnp.tile(q_norm * Q_PRESCALE, n_heads).reshape(1, D_MODEL)
    slopes = jnp.exp2(-8.0 * (jnp.arange(n_heads, dtype=F32) + 1.0) / n_heads) * LOG2_E
    s1 = slopes.astype(BF16).astype(F32)
    s2 = (slopes - s1).astype(BF16).astype(F32)
    s3 = (slopes - s1 - s2).astype(BF16).astype(F32)
    terms = jnp.stack([s1, s2, s3], axis=1)
    qc = jnp.concatenate([jnp.zeros((n_heads, dh), F32), terms, terms * ATT_TK,
                          jnp.zeros((n_heads, LANES - dh - 2 * N_POS_LANES), F32)], axis=1).reshape(1, QA_COLS)

    wkv = w_in[:, nq:nq + 6 * kv_cols].reshape(D_MODEL, 6, hkv, dh)
    wkva = jnp.stack([wkv[:, 2], wkv[:, 4], wkv[:, 3], wkv[:, 5]], axis=2).reshape(D_MODEL, 4 * kv_cols)
    wkvc = jnp.stack([wkv[:, 0], wkv[:, 1]], axis=2).reshape(D_MODEL, KVC_COLS)
    wg = _pad_cols(w_in[:, nq + 6 * kv_cols:], LANES)
    kg = jnp.tile(jnp.concatenate([k_norm[1], k_norm[2]]), hkv).reshape(1, hkv * LANES)
    q, kva, kvc, gt = _nsa_proj(h, seq, gn.reshape(1, D_MODEL), _bf(wq), _bf(wkva), _bf(wkvc), _bf(wg), qg, qc, kg)

    w1 = cmp_w1.reshape(2, 2 * CMP_STRIDE, dh, CMP_HID)
    zeros = jnp.zeros((2 * CMP_STRIDE, dh, CMP_HID), F32)
    wfull = jnp.concatenate([jnp.concatenate([w1[0], zeros], axis=2),
                             jnp.concatenate([zeros, w1[1]], axis=2)], axis=1)
    pe = jnp.concatenate([cmp_pe[0], cmp_pe[1]], axis=1)
    w2 = jnp.zeros((2 * CMP_HID, 2 * LANES), F32)
    w2 = w2.at[:CMP_HID, :dh].set(cmp_w2[0]).at[CMP_HID:, LANES:LANES + dh].set(cmp_w2[1])
    kg0 = jnp.concatenate([k_norm[0], one, one, one]).reshape(1, 2 * LANES)
    cmp = _nsa_cmp(batch, seq, kvc, _bf(wfull[:CMP_STRIDE]), _bf(wfull[CMP_STRIDE:]),
                   pe[:CMP_STRIDE], pe[CMP_STRIDE:], _bf(w2), kg0)

    key = jnp.arange(seq + ATT_TK)[:, None]
    lanes = jnp.arange(LANES)[None, :]
    onehot = jnp.where(key < seq, lanes == key // SLC_BLOCK, lanes == PAD_TILE_LANE)
    return _nsa_attn(batch, seq, q, kva, _bf(jnp.where(onehot, MASK_NEG, 0.0)), cmp, gt)


def kernel(x, mix_norm, mlp_norm, mlp_w1, mlp_w2, rwkv_mu, rwkv_w_in, rwkv_w0, rwkv_w1, rwkv_w2, rwkv_a0, rwkv_a1, rwkv_a2, rwkv_v0, rwkv_v1, rwkv_v2, rwkv_g1, rwkv_g2, rwkv_k_k, rwkv_k_a, rwkv_r_k, rwkv_ln_w, rwkv_ln_b, rwkv_w_out, nsa_w_in, nsa_cmp_pe, nsa_cmp_w1, nsa_cmp_w2, nsa_q_norm, nsa_k_norm, nsa_w_out):
    batch, seq, d = x.shape
    depth = mix_norm.shape[0]
    assert d == D_MODEL and RWKV_N == HEAD_DIM and NSA_DH == HEAD_DIM
    assert seq % ATT_TQ == 0 and seq >= (WINDOW // ATT_TK + 1) * ATT_TK and seq // SLC_BLOCK < PAD_TILE_LANE
    assert seq % WKV_TBLOCK == 0 and seq % RW_TM == 0 and (batch * seq) % MLP_TM == 0
    h = x.reshape(batch * seq, d)
    v_first = None
    for i in range(depth):
        j = i // 2
        if i % 2 == 0:
            vres = None if j == 0 else (rwkv_v0[j - 1], rwkv_v1[j - 1], rwkv_v2[j - 1])
            a, v_raw = _rwkv_layer(h, batch, seq, mix_norm[i], rwkv_mu[j], rwkv_w_in[j], rwkv_w0[j], rwkv_w1[j],
                                   rwkv_w2[j], rwkv_a0[j], rwkv_a1[j], rwkv_a2[j], rwkv_g1[j], rwkv_g2[j],
                                   rwkv_k_k[j], rwkv_k_a[j], rwkv_r_k[j].reshape(-1), rwkv_ln_w[j], rwkv_ln_b[j],
                                   v_first, vres)
            if j == 0:
                v_first = v_raw
            wo = rwkv_w_out[j]
        else:
            a = _nsa_layer(h, batch, seq, mix_norm[i], nsa_w_in[j], nsa_cmp_pe[j], nsa_cmp_w1[j], nsa_cmp_w2[j],
                           nsa_q_norm[j], nsa_k_norm[j])
            wo = nsa_w_out[j]
        h = _out_mlp(h, a, _bf(wo), mlp_norm[i].reshape(1, d), _bf(mlp_w1[i]), _bf(mlp_w2[i]))
    return h.reshape(batch, seq, d)
```

```python
import functools

import jax
import jax.numpy as jnp
import numpy as np
from jax import lax
from jax.experimental import pallas as pl
from jax.experimental.pallas import tpu as pltpu

F32 = jnp.float32
BF16 = jnp.bfloat16

D_MODEL = 1024
MLP_HIDDEN = 4 * D_MODEL
NORM_EPS = 1e-6
LANES = 128
SUBLANES = 8
HEAD_DIM = 64
HEAD_SHIFT = HEAD_DIM.bit_length() - 1
VMEM_LIMIT = 56 * 1024 * 1024

RWKV_N = 64
RWKV_GN_EPS = 64e-5
WKV_CHUNK = 64
WKV_TBLOCK = 256
WKV_PAIRS = 8

NSA_HKV = 4
NSA_G = 4
NSA_DH = 64
CMP_STRIDE = 16
CMP_HID = 128
SLC_BLOCK = 64
N_SELECT = 16
WINDOW = 512
SEL_BIG = 1e9
MASK_NEG = -1e30
LOG2_E = 1.4426950408889634
Q_PRESCALE = NSA_DH ** -0.5 * LOG2_E
PAD_TILE_LANE = LANES - 1


def _dot(a, b):
    return jnp.dot(a, b, preferred_element_type=F32)


def _dot_nt(a, b):
    return lax.dot_general(a, b, (((1,), (1,)), ((), ())), preferred_element_type=F32)


def _bf(x):
    return x.astype(BF16)


def _split2(x):
    hi = x.astype(BF16)
    lo = (x - hi.astype(F32)).astype(BF16)
    return hi, lo


def _dot_hl(x, w):
    hi, lo = _split2(x)
    return _dot(hi, w) + _dot(lo, w)


def _dot_hl_rev(w, x):
    hi, lo = _split2(x)
    return _dot(w, hi) + _dot(w, lo)


def _iota(shape, dim):
    return lax.broadcasted_iota(jnp.int32, shape, dim)


def _bd64(n=LANES):
    return jnp.where((_iota((n, n), 0) >> HEAD_SHIFT) == (_iota((n, n), 1) >> HEAD_SHIFT), 1.0, 0.0).astype(BF16)


def _seg64_sum(x, bd):
    outs = [_dot(_bf(x[:, c * LANES:(c + 1) * LANES]), bd) for c in range(x.shape[1] // LANES)]
    return outs[0] if len(outs) == 1 else jnp.concatenate(outs, axis=1)


def _rms(x, g):
    return x * lax.rsqrt(jnp.mean(x * x, axis=-1, keepdims=True) + NORM_EPS) * g


def _const_spec(shape):
    return pl.BlockSpec(shape, lambda *_: (0,) * len(shape))


def _params(sem):
    return pltpu.CompilerParams(dimension_semantics=sem, vmem_limit_bytes=VMEM_LIMIT)


MLP_TM = 512
MLP_TH = 1024


def _out_mlp_kernel(h_ref, a_ref, wo_ref, g_ref, w1_ref, w2_ref, o_ref):
    h1 = h_ref[...] + _dot(a_ref[...], wo_ref[...])
    xb = _bf(_rms(h1, g_ref[...]))
    acc = h1
    for c in range(MLP_HIDDEN // MLP_TH):
        u = jnp.maximum(_dot(xb, w1_ref[:, c * MLP_TH:(c + 1) * MLP_TH]), 0.0)
        acc = acc + _dot(_bf(u * u), w2_ref[c * MLP_TH:(c + 1) * MLP_TH, :])
    o_ref[...] = acc


def _out_mlp(h, a, wo, g, w1, w2):
    m = h.shape[0]
    row = lambda i: (i, 0)
    return pl.pallas_call(
        _out_mlp_kernel,
        grid=(m // MLP_TM,),
        in_specs=[pl.BlockSpec((MLP_TM, D_MODEL), row), pl.BlockSpec((MLP_TM, D_MODEL), row),
                  _const_spec((D_MODEL, D_MODEL)), _const_spec((1, D_MODEL)),
                  _const_spec((D_MODEL, MLP_HIDDEN)), _const_spec((MLP_HIDDEN, D_MODEL))],
        out_specs=pl.BlockSpec((MLP_TM, D_MODEL), row),
        out_shape=jax.ShapeDtypeStruct((m, D_MODEL), F32),
        compiler_params=_params(("parallel",)),
    )(h, a, wo, g, w1, w2)


RW_TM = 512
SHIFT_ROWS = 8


def _rwkv_prep_kernel(*refs, tiles_per_seq, vres):
    if vres:
        (h_ref, hp_ref, gn_ref, mu_ref, win_ref, w0_ref, w1_ref, w2_ref, a0_ref, a1_ref, a2_ref,
         g1_ref, g2_ref, kk_ref, ka_ref, vf_ref, v0_ref, v1_ref, v2_ref,
         r_o, lw_o, k_o, v_o, an_o, bb_o, g_o) = refs
    else:
        (h_ref, hp_ref, gn_ref, mu_ref, win_ref, w0_ref, w1_ref, w2_ref, a0_ref, a1_ref, a2_ref,
         g1_ref, g2_ref, kk_ref, ka_ref,
         r_o, lw_o, k_o, v_o, an_o, bb_o, g_o) = refs
    gn = gn_ref[...]
    xn = _rms(h_ref[...], gn)
    first = (pl.program_id(0) % tiles_per_seq) == 0
    prev = _rms(hp_ref[SHIFT_ROWS - 1:SHIFT_ROWS, :], gn)
    prev = jnp.where(first, 0.0, prev)
    xs = pltpu.roll(xn, 1, axis=0)
    xs = jnp.where(_iota(xn.shape, 0) == 0, prev, xs)
    dx = xs - xn

    def mixed(i):
        return _bf(xn + dx * mu_ref[i:i + 1, :])

    xa = mixed(4)
    a_low = _dot(xa, a1_ref[...])
    r_o[...] = _dot(mixed(0), win_ref[:, 0:D_MODEL])
    a = jax.nn.sigmoid(a0_ref[...] + _dot(_bf(a_low), a2_ref[...]))
    k = _dot(mixed(2), win_ref[:, D_MODEL:2 * D_MODEL])
    xw = mixed(1)
    w_low = jnp.tanh(_dot(xw, w1_ref[...]))
    kk = k * kk_ref[...]
    nrm = jnp.sqrt(_seg64_sum(kk * kk, _bd64()))
    k_o[...] = k * (1.0 + (a - 1.0) * ka_ref[...])
    z = w0_ref[...] + _dot(_bf(w_low), w2_ref[...])
    kk = kk / jnp.maximum(nrm, 1e-12)
    an_o[...] = -kk
    bb_o[...] = kk * a
    xv = mixed(3)
    v = _dot(xv, win_ref[:, 2 * D_MODEL:3 * D_MODEL])
    softplus = jnp.maximum(-z, 0.0) + jnp.log(1.0 + jnp.exp(-jnp.abs(z)))
    lw = -jnp.exp(-softplus - 0.5)
    if vres:
        lo = _dot(_bf(_dot(xv, v1_ref[...])), v2_ref[...])
        v = v + (vf_ref[...] - v) * jax.nn.sigmoid(v0_ref[...] + lo)
    v_o[...] = v
    g_low = jax.nn.sigmoid(_dot(mixed(5), g1_ref[...]))
    ii = _iota((RW_TM, RW_TM), 0)
    jj = _iota((RW_TM, RW_TM), 1)
    sh = WKV_CHUNK.bit_length() - 1
    cum_mat = jnp.where(((ii >> sh) == (jj >> sh)) & (jj <= ii), 1.0, 0.0).astype(BF16)
    lw_o[...] = _dot_hl_rev(cum_mat, lw)
    g_o[...] = _dot(_bf(g_low), g2_ref[...])


def _rwkv_prep(h, seq, gn, mu, win, w0, w1, w2, a0, a1, a2, g1, g2, k_k, k_a, vres):
    m = h.shape[0]
    row = lambda i: (i, 0)
    prev = lambda i: (jnp.maximum(i * (RW_TM // SHIFT_ROWS) - 1, 0), 0)
    vec = _const_spec((1, D_MODEL))
    full = lambda a: _const_spec(a.shape)
    args = [h, h, gn, mu, win, w0, w1, w2, a0, a1, a2, g1, g2, k_k, k_a]
    specs = [pl.BlockSpec((RW_TM, D_MODEL), row), pl.BlockSpec((SHIFT_ROWS, D_MODEL), prev), vec, full(mu),
             full(win), vec, full(w1), full(w2), vec, full(a1), full(a2), full(g1), full(g2), vec, vec]
    if vres is not None:
        v_first, v0, v1, v2 = vres
        args += [v_first, v0, v1, v2]
        specs += [pl.BlockSpec((RW_TM, D_MODEL), row), vec, full(v1), full(v2)]
    out = jax.ShapeDtypeStruct((m, D_MODEL), F32)
    return pl.pallas_call(
        functools.partial(_rwkv_prep_kernel, tiles_per_seq=seq // RW_TM, vres=vres is not None),
        grid=(m // RW_TM,),
        in_specs=specs,
        out_specs=[pl.BlockSpec((RW_TM, D_MODEL), row)] * 7,
        out_shape=[out] * 7,
        compiler_params=_params(("parallel",)),
    )(*args)


def _wkv_kernel(r_ref, cum_ref, k_ref, v_ref, a_ref, b_ref, g_ref, rk_ref, lnw_ref, lnb_ref,
                o_ref, s_ref, y_ref):
    L = WKV_CHUNK
    L2 = 2 * L

    @pl.when(pl.program_id(2) == 0)
    def _():
        s_ref[...] = jnp.zeros_like(s_ref)

    head0 = _iota((1, LANES), 1) < RWKV_N
    ii = _iota((L2, L2), 0)
    jj = _iota((L2, L2), 1)
    same = (ii >= L) == (jj >= L)
    tril_strict = same & (jj < ii)
    tril_incl = same & (jj <= ii)
    eye = jnp.where(ii == jj, 1.0, 0.0)
    first_row = _iota((L, LANES), 0) == 0
    bd = (_iota((LANES, LANES), 0) >> HEAD_SHIFT) == (_iota((LANES, LANES), 1) >> HEAD_SHIFT)

    def by_head(x):
        return jnp.concatenate([jnp.where(head0, x, 0.0), jnp.where(head0, 0.0, x)], axis=0)

    def twice(x):
        return jnp.concatenate([x, x], axis=0)

    def pick(x2):
        return jnp.where(head0, x2[:L], x2[L:])

    def chunk(sl, cols, s):
        cum = cum_ref[sl, cols]
        cum_ex = jnp.where(first_row, 0.0, pltpu.roll(cum, 1, axis=0))
        g_in = jnp.exp(cum)
        g_inv = jnp.exp(-cum)
        cum_l = cum[L - 1:L, :]
        g_to_end = jnp.exp(cum_l - cum)
        r = r_ref[sl, cols]
        k = k_ref[sl, cols]
        v = v_ref[sl, cols]
        b = b_ref[sl, cols]
        at = a_ref[sl, cols] * jnp.exp(cum_ex)
        rt = r * g_in
        bt = b * g_inv
        kt = k * g_inv
        sc = _dot_nt(_bf(jnp.concatenate([by_head(at), by_head(rt)], axis=0)),
                     _bf(jnp.concatenate([bt, bt, kt, kt], axis=0)))
        yield
        m_ab = jnp.where(tril_strict, sc[0:L2, 0:L2], 0.0)
        m_ak = jnp.where(tril_strict, sc[0:L2, L2:2 * L2], 0.0)
        n_rb = jnp.where(tril_incl, sc[L2:2 * L2, 0:L2], 0.0)
        n_rk = jnp.where(tril_incl, sc[L2:2 * L2, L2:2 * L2], 0.0)
        inv = eye + m_ab
        pw = _dot(_bf(m_ab), _bf(m_ab))
        yield
        for _ in range(int(np.log2(L)) - 2):
            both = _dot(_bf(jnp.concatenate([pw, inv], axis=0)), _bf(pw))
            yield
            pw, inv = both[0:L2], inv + both[L2:2 * L2]
        inv = inv + _dot(_bf(inv), _bf(pw))
        yield
        ps = _dot_nt(_bf(jnp.concatenate([at, rt], axis=0)), _bf(s))
        yield
        mv = _dot(_bf(jnp.concatenate([m_ak, n_rk], axis=0)), _bf(twice(v)))
        yield
        u = pick(_dot(_bf(inv), _bf(twice(ps[:L]) + mv[0:L2])))
        yield
        y_ref[sl, cols] = pick(twice(ps[L:]) + _dot(_bf(n_rb), _bf(twice(u))) + mv[L2:2 * L2])
        yield
        uv = jnp.concatenate([u, v], axis=0)
        bk = jnp.concatenate([b * g_to_end, k * g_to_end], axis=0)
        return s * jnp.exp(cum_l) + jnp.where(bd, _dot(_bf(uv.T), _bf(bk)), 0.0)

    ones = jnp.where(bd, 1.0, 0.0).astype(BF16)

    def pair(p):
        cols = slice(p * LANES, (p + 1) * LANES)
        s = s_ref[p]
        for c in range(WKV_TBLOCK // L):
            s = yield from chunk(pl.ds(c * L, L), cols, s)
            yield
        s_ref[p] = s
        y = y_ref[:, cols]
        mean = _dot_hl(y, ones) * (1.0 / RWKV_N)
        yield
        d = y - mean
        var = _dot(_bf(d * d), ones) * (1.0 / RWKV_N)
        yield
        yn = d * lax.rsqrt(var + RWKV_GN_EPS) * lnw_ref[:, cols] + lnb_ref[:, cols]
        bonus = _dot(_bf(r_ref[:, cols] * k_ref[:, cols] * rk_ref[:, cols]), ones) * v_ref[:, cols]
        o_ref[:, cols] = _bf((yn + bonus) * g_ref[:, cols])

    running = [pair(p) for p in range(WKV_PAIRS)]
    done = object()
    while running:
        running = [gen for gen in running if next(gen, done) is not done]


def _wkv(batch, seq, r, cum, k, v, an, bb, g, r_k, ln_w, ln_b):
    m = r.shape[0]
    nt = seq // WKV_TBLOCK
    width = WKV_PAIRS * LANES
    blk = pl.BlockSpec((WKV_TBLOCK, width), lambda b, p, i: (b * nt + i, p))
    vec = pl.BlockSpec((1, width), lambda b, p, i: (0, p))
    return pl.pallas_call(
        _wkv_kernel,
        grid=(batch, D_MODEL // width, nt),
        in_specs=[blk] * 7 + [vec] * 3,
        out_specs=blk,
        out_shape=jax.ShapeDtypeStruct((m, D_MODEL), BF16),
        scratch_shapes=[pltpu.VMEM((WKV_PAIRS, LANES, LANES), F32), pltpu.VMEM((WKV_TBLOCK, width), F32)],
        compiler_params=_params(("parallel", "parallel", "arbitrary")),
    )(r, cum, k, v, an, bb, g, r_k, ln_w, ln_b)


ATT_TQ = 256
ATT_TK = 256
NSA_TM = ATT_TK
N_POS_LANES = 3
KVA_HEAD_COLS = 4 * LANES
KVA_COLS = NSA_HKV * KVA_HEAD_COLS
KVC_COLS = NSA_HKV * LANES
QA_COLS = NSA_HKV * NSA_G * LANES
GATE_COLS = NSA_HKV * LANES


def _nsa_proj_kernel(h_ref, gn_ref, wq_ref, wkv_ref, wkvc_ref, wg_ref, qg_ref, qc_ref, kg_ref,
                     q_o, kva_o, kvc_o, gt_o, *, tiles_per_seq):
    xb = _bf(_rms(h_ref[...], gn_ref[...]))
    bd = _bd64()
    inv_dh = 1.0 / NSA_DH
    lane = _iota((NSA_TM, LANES), 1)
    low = lane < NSA_DH

    def spread(pair, fill_a, fill_b):
        return [jnp.where(low, pair, fill_a), jnp.where(low, pltpu.roll(pair, NSA_DH, axis=1), fill_b)]

    q = _dot(xb, wq_ref[...])
    q = q * lax.rsqrt(_seg64_sum(q * q, bd) * inv_dh + NORM_EPS) * qg_ref[...]
    outs = []
    for c in range(D_MODEL // LANES):
        outs += spread(q[:, c * LANES:(c + 1) * LANES], qc_ref[:, 2 * c * LANES:(2 * c + 1) * LANES],
                       qc_ref[:, (2 * c + 1) * LANES:(2 * c + 2) * LANES])
    q_o[...] = _bf(jnp.concatenate(outs, axis=1))

    pos_mod = _iota((NSA_TM, LANES), 0).astype(F32)
    pos_tile = (pl.program_id(0) % tiles_per_seq).astype(F32)
    key_fill = jnp.where(lane < NSA_DH + N_POS_LANES, pos_mod, jnp.where(lane < NSA_DH + 2 * N_POS_LANES, pos_tile, 0.0))
    val_fill = jnp.where(lane == NSA_DH, 1.0, 0.0)
    kv = _dot(xb, wkv_ref[...])
    outs = []
    for hd in range(NSA_HKV):
        keys = kv[:, 2 * hd * LANES:(2 * hd + 1) * LANES]
        keys = keys * lax.rsqrt(_dot(_bf(keys * keys), bd) * inv_dh + NORM_EPS) * kg_ref[:, hd * LANES:(hd + 1) * LANES]
        outs += spread(keys, key_fill, key_fill)
        outs += spread(kv[:, (2 * hd + 1) * LANES:(2 * hd + 2) * LANES], val_fill, val_fill)
    kva_o[...] = _bf(jnp.concatenate(outs, axis=1))

    kvc_o[...] = _dot(xb, wkvc_ref[...])
    gates = jax.nn.sigmoid(_dot(xb, wg_ref[...]))
    per_head = 3 * NSA_G
    gt_o[...] = jnp.concatenate([gates if hd == 0 else pltpu.roll(gates, LANES - per_head * hd, axis=1)
                                 for hd in range(NSA_HKV)], axis=1)


def _nsa_proj(h, seq, gn, wq, wkv, wkvc, wg, qg, qc, kg):
    m = h.shape[0]
    row = lambda i: (i, 0)
    return pl.pallas_call(
        functools.partial(_nsa_proj_kernel, tiles_per_seq=seq // NSA_TM),
        grid=(m // NSA_TM,),
        in_specs=[pl.BlockSpec((NSA_TM, D_MODEL), row), _const_spec((1, D_MODEL)),
                  _const_spec(wq.shape), _const_spec(wkv.shape), _const_spec(wkvc.shape), _const_spec(wg.shape),
                  _const_spec(qg.shape), _const_spec(qc.shape), _const_spec(kg.shape)],
        out_specs=[pl.BlockSpec((NSA_TM, QA_COLS), row), pl.BlockSpec((NSA_TM, KVA_COLS), row),
                   pl.BlockSpec((NSA_TM, KVC_COLS), row), pl.BlockSpec((NSA_TM, GATE_COLS), row)],
        out_shape=[jax.ShapeDtypeStruct((m, QA_COLS), BF16), jax.ShapeDtypeStruct((m, KVA_COLS), BF16),
                   jax.ShapeDtypeStruct((m, KVC_COLS), F32), jax.ShapeDtypeStruct((m, GATE_COLS), F32)],
        compiler_params=_params(("parallel",)),
    )(h, gn, wq, wkv, wkvc, wg, qg, qc, kg)


def _nsa_cmp_kernel(kv_ref, wa_ref, wb_ref, pa_ref, pb_ref, w2_ref, kg_ref, o_ref, *, n_blk):
    nb = n_blk + 1
    first = jnp.zeros((nb, 2 * CMP_HID), F32)
    second = jnp.zeros((nb, 2 * CMP_HID), F32)
    for p in range(CMP_STRIDE):
        x = kv_ref[pl.ds(p, nb, stride=CMP_STRIDE), :]
        first = first + _dot(_bf(x + pa_ref[p:p + 1, :]), wa_ref[p])
        second = second + _dot(_bf(x + pb_ref[p:p + 1, :]), wb_ref[p])
    hid = jax.nn.gelu(first + pltpu.roll(second, nb - 1, axis=0))
    out = _dot(_bf(hid), w2_ref[...])
    ss = _seg64_sum(out * out, _bd64()) * (1.0 / NSA_DH)
    lane = _iota((nb, 2 * LANES), 1)
    out = jnp.where(lane < NSA_DH, out * lax.rsqrt(ss + NORM_EPS) * kg_ref[...], out)
    end = _iota((nb, 2 * LANES), 0) * CMP_STRIDE + (2 * CMP_STRIDE - 1)
    sh = ATT_TK.bit_length() - 1
    out = jnp.where((lane >= NSA_DH) & (lane < NSA_DH + N_POS_LANES), (end & (ATT_TK - 1)).astype(F32), out)
    out = jnp.where((lane >= NSA_DH + N_POS_LANES) & (lane < NSA_DH + 2 * N_POS_LANES), (end >> sh).astype(F32), out)
    o_ref[...] = _bf(out)


def _nsa_cmp(batch, seq, kv, wa, wb, pa, pb, w2, kg):
    n_blk = seq // CMP_STRIDE - 1
    nb = n_blk + 1
    return pl.pallas_call(
        functools.partial(_nsa_cmp_kernel, n_blk=n_blk),
        grid=(batch, NSA_HKV),
        in_specs=[pl.BlockSpec((seq, LANES), lambda b, h: (b, h)),
                  _const_spec(wa.shape), _const_spec(wb.shape), _const_spec(pa.shape), _const_spec(pb.shape),
                  _const_spec(w2.shape), _const_spec(kg.shape)],
        out_specs=pl.BlockSpec((None, None, nb, 2 * LANES), lambda b, h: (b, h, 0, 0)),
        out_shape=jax.ShapeDtypeStruct((batch, NSA_HKV, nb, 2 * LANES), BF16),
        compiler_params=_params(("parallel", "parallel")),
    )(kv, wa, wb, pa, pb, w2, kg)


def _nsa_attn_kernel(q_ref, kv_ref, oh_ref, cmp_ref, gt_ref, o_ref,
                     qam_s, m_s, acc_s, sa_s, sb_s, ow_s, tiles_s, *, n_blk, n_sel, n_slc):
    tq, tk = ATT_TQ, ATT_TK
    R = NSA_G * tq
    qi = pl.program_id(2)
    q0 = qi * tq
    t_one = q0 + _iota((tq, 1), 0)
    gt = gt_ref[...]
    n_ch = NSA_G
    qa = jnp.concatenate([q_ref[:, g * LANES:(g + 1) * LANES] for g in range(NSA_G)], axis=0)
    qam_s[:, 0:LANES] = qa

    n_win = WINDOW // tk + 1
    wrows = pl.ds(pl.multiple_of(jnp.maximum(qi - (n_win - 1), 0) * tk, tk), n_win * tk)

    def window():
        dist = t_one - (wrows.start + _iota((1, n_win * tk), 1))
        in_window = dist.astype(jnp.uint32) < WINDOW
        s_all = _dot_nt(qa, kv_ref[wrows, LANES:2 * LANES])
        yield
        pes = []
        for c in range(n_ch):
            s = jnp.where(in_window, s_all[c * tq:(c + 1) * tq], MASK_NEG)
            pes.append(_bf(jnp.exp2(s - jnp.max(s, axis=-1, keepdims=True))))
            yield
        pv = _dot(jnp.concatenate(pes, axis=0), kv_ref[wrows, 3 * LANES:4 * LANES]).reshape(NSA_G, tq, LANES)
        yield
        for g in range(NSA_G):
            ow_s[g] = ((gt[:, 3 * g + 2:3 * g + 3] / pv[g][:, NSA_DH:NSA_DH + 1]) * pv[g][:, 0:NSA_DH]
                       + gt[:, 3 * g:3 * g + 1] * picked["o_cmp"][g * tq:(g + 1) * tq])

    picked = {}

    def compressed():
        nb = cmp_ref.shape[0]
        n_idx = _iota((1, nb), 1)
        ps = []
        valid = (n_idx * CMP_STRIDE + (2 * CMP_STRIDE - 1) <= t_one) & (n_idx < n_blk)
        qk = _dot_nt(qa, cmp_ref[:, 0:LANES])
        yield
        for g in range(NSA_G):
            s = jnp.where(valid, qk[g * tq:(g + 1) * tq], -jnp.inf)
            mx = jnp.max(s, axis=-1, keepdims=True)
            mx = jnp.where(mx == -jnp.inf, 0.0, mx)
            e = jnp.exp2(s - mx)
            ps.append(e * (1.0 / jnp.maximum(jnp.sum(e, axis=-1, keepdims=True), 1e-30)))
            yield
        picked["o_cmp"] = _dot(_bf(jnp.concatenate(ps, axis=0)), cmp_ref[:, LANES:LANES + NSA_DH])
        yield
        psum = ps[0] + ps[1] + ps[2] + ps[3]
        sj = _iota((n_slc, nb), 0) * SLC_BLOCK
        cn = _iota((n_slc, nb), 1) * CMP_STRIDE
        overlap_t = jnp.where((cn <= sj + SLC_BLOCK - 1) & (cn + 2 * CMP_STRIDE - 1 >= sj), 1.0, 0.0).astype(BF16)
        p_hi, p_lo = _split2(psum)
        imp = _dot_nt(overlap_t, p_hi) + _dot_nt(overlap_t, p_lo)
        yield
        blk = _iota((n_slc, 1), 0)
        ahead_of_cur = ((q0 + _iota((1, tq), 1)) >> (SLC_BLOCK.bit_length() - 1)) - blk
        forced = (blk * ahead_of_cur * (ahead_of_cur - 1)) == 0
        imp = jnp.where(forced, SEL_BIG, imp)
        imp = jnp.where(ahead_of_cur >= 0, imp, -SEL_BIG)
        sub = SUBLANES
        ranks = []
        for lo in range(0, n_slc, sub):
            mine = imp[lo:lo + sub, :]
            rk = jnp.zeros((sub, tq), F32)
            for j in range(n_slc):
                other = imp[j:j + 1, :]
                if j < lo:
                    rk = rk + jnp.where(other >= mine, 1.0, 0.0)
                elif j >= lo + sub:
                    rk = rk + jnp.where(other > mine, 1.0, 0.0)
                else:
                    rk = rk + jnp.where(blk[lo:lo + sub] > j, jnp.where(other >= mine, 1.0, 0.0),
                                        jnp.where(other > mine, 1.0, 0.0))
            ranks.append(rk)
            yield
        rank = jnp.concatenate(ranks, axis=0)
        picked["not_sel_t"] = jnp.where(rank < n_sel, 0.0, 1.0)

    branches = {"w": window(), "c": compressed()}
    for name in "c" + "w" + "cccc" + "cc" + "wwww" + "w" + "c" * (n_slc // 8) + "cw":
        next(branches[name], None)
    assert all(next(branch, "finished") == "finished" for branch in branches.values())
    not_sel_t = picked["not_sel_t"]
    qm = jnp.concatenate([not_sel_t.T, jnp.zeros((tq, LANES - n_slc), F32)], axis=1)
    qm = _bf(jnp.where(_iota((tq, LANES), 1) == PAD_TILE_LANE, 1.0, qm))
    for g in range(NSA_G):
        qam_s[g * tq:(g + 1) * tq, LANES:2 * LANES] = qm

    n_kt = n_slc * SLC_BLOCK // tk
    in_tile = (_iota((n_kt, n_slc), 1) * SLC_BLOCK // tk) == _iota((n_kt, n_slc), 0)
    picks = _dot(jnp.where(in_tile, 1.0, 0.0).astype(BF16), _bf(1.0 - not_sel_t))
    needed = jnp.max(picks, axis=1, keepdims=True)
    n_int = jnp.int32(0)
    for kt in range(n_kt):
        tiles_s[n_int] = kt
        n_int = n_int + ((needed[kt, 0] > 0.0) & (kt < qi)).astype(jnp.int32)
    n_even = n_int + (n_int & 1)
    tiles_s[n_int] = n_kt
    tiles_s[n_even] = qi
    tiles_s[n_even + 1] = qi

    m_s[...] = jnp.full_like(m_s, MASK_NEG)
    acc_s[...] = jnp.zeros_like(acc_s)

    def key_rows(j):
        kt = tiles_s[j]
        return (pl.ds(pl.multiple_of(jnp.minimum(kt, n_kt - 1) * tk, tk), tk),
                pl.ds(pl.multiple_of(kt * tk, tk), tk))

    def scores(j, s_ref):
        kv_rows, oh_rows = key_rows(j)
        kaug = jnp.concatenate([kv_ref[kv_rows, 0:LANES], oh_ref[oh_rows, :]], axis=1)
        s_ref[...] = _dot_nt(qam_s[...], kaug)

    def softmax_pv(j, s_ref, diagonal):
        kv_rows, _ = key_rows(j)
        pes, alphas = [], []
        causal = q0 + _iota((1, tk), 1) <= t_one
        for c in range(n_ch):
            s = s_ref[c * tq:(c + 1) * tq, :]
            if diagonal:
                s = jnp.where(causal, s, MASK_NEG)
            m_old = m_s[c]
            m_new = jnp.maximum(m_old, jnp.max(s, axis=-1, keepdims=True))
            pes.append(_bf(jnp.exp2(s - jnp.concatenate([m_new] * (tk // LANES), axis=1))))
            alphas.append(jnp.exp2(m_old - m_new))
            m_s[c] = m_new
        pv = _dot(jnp.concatenate(pes, axis=0), kv_ref[kv_rows, 2 * LANES:3 * LANES])
        for c in range(n_ch):
            acc_s[c] = alphas[c] * acc_s[c] + pv[c * tq:(c + 1) * tq]

    def sel_body(jj, carry):
        j = 2 * jj
        scores(j + 1, sb_s)
        softmax_pv(j, sa_s, False)
        scores(j + 2, sa_s)
        softmax_pv(j + 1, sb_s, False)
        return carry

    scores(0, sa_s)
    lax.fori_loop(0, n_even // 2, sel_body, 0)
    softmax_pv(n_even, sa_s, True)
    acc = acc_s[...].reshape(NSA_G, tq, LANES)

    outs = []
    for g in range(NSA_G):
        w_sel = gt[:, 3 * g + 1:3 * g + 2] / acc[g][:, NSA_DH:NSA_DH + 1]
        outs.append(w_sel * acc[g][:, 0:NSA_DH] + ow_s[g])
    o_ref[...] = _bf(jnp.concatenate(outs, axis=1))


def _nsa_attn(batch, seq, q, kva, onehot, cmp, gt):
    assert ATT_TQ == ATT_TK
    m = q.shape[0]
    nq = seq // ATT_TQ
    n_blk = seq // CMP_STRIDE - 1
    n_slc = seq // SLC_BLOCK
    n_sel = min(N_SELECT, n_slc)
    R = NSA_G * ATT_TQ
    tile = lambda b, h, i: (b * nq + i, h)
    return pl.pallas_call(
        functools.partial(_nsa_attn_kernel, n_blk=n_blk, n_sel=n_sel, n_slc=n_slc),
        grid=(batch, NSA_HKV, nq),
        in_specs=[pl.BlockSpec((ATT_TQ, NSA_G * LANES), tile),
                  pl.BlockSpec((seq, KVA_HEAD_COLS), lambda b, h, i: (b, h)),
                  pl.BlockSpec((seq + ATT_TK, LANES), lambda b, h, i: (0, 0)),
                  pl.BlockSpec((None, None, n_blk + 1, 2 * LANES), lambda b, h, i: (b, h, 0, 0)),
                  pl.BlockSpec((ATT_TQ, LANES), tile)],
        out_specs=pl.BlockSpec((ATT_TQ, NSA_G * NSA_DH), tile),
        out_shape=jax.ShapeDtypeStruct((m, D_MODEL), BF16),
        scratch_shapes=[pltpu.VMEM((R, 2 * LANES), BF16),
                        pltpu.VMEM((NSA_G, ATT_TQ, LANES), F32),
                        pltpu.VMEM((NSA_G, ATT_TQ, LANES), F32),
                        pltpu.VMEM((R, ATT_TK), F32), pltpu.VMEM((R, ATT_TK), F32),
                        pltpu.VMEM((NSA_G, ATT_TQ, NSA_DH), F32),
                        pltpu.SMEM((seq // ATT_TK + 3,), jnp.int32)],
        compiler_params=_params(("parallel", "parallel", "arbitrary")),
    )(q, kva, onehot, cmp, gt)


def _pad_cols(w, n):
    return jnp.pad(w, ((0, 0), (0, n - w.shape[1])))


def _pad_rows(w, n):
    return jnp.pad(w, ((0, n - w.shape[0]), (0, 0)))


def _lora(w_down, w_up):
    rank = -(-w_down.shape[1] // LANES) * LANES
    return _bf(_pad_cols(w_down, rank)), _bf(_pad_rows(w_up, rank))


def _rwkv_layer(h, batch, seq, gn, mu, w_in, w0, w1, w2, a0, a1, a2, g1, g2, k_k, k_a, r_k, ln_w, ln_b,
                v_first, vres):
    vec = lambda x: x.reshape(1, D_MODEL)
    w1p, w2p = _lora(w1, w2)
    a1p, a2p = _lora(a1, a2)
    g1p, g2p = _lora(g1, g2)
    if vres is not None:
        v0, v1, v2 = vres
        v1p, v2p = _lora(v1, v2)
        vres = (v_first, vec(v0), v1p, v2p)
    mu8 = jnp.pad(mu, ((0, 2), (0, 0)))
    r, lw, k, v, an, bb, g = _rwkv_prep(h, seq, vec(gn), mu8, _bf(w_in), vec(w0), w1p, w2p, vec(a0), a1p, a2p,
                                        g1p, g2p, vec(k_k), vec(k_a), vres)
    y = _wkv(batch, seq, r, lw, k, v, an, bb, g, vec(r_k), vec(ln_w), vec(ln_b))
    return y, v


def _nsa_layer(h, batch, seq, gn, w_in, cmp_pe, cmp_w1, cmp_w2, q_norm, k_norm):
    hkv, dh = NSA_HKV, NSA_DH
    nq = D_MODEL
    n_heads = hkv * NSA_G
    kv_cols = hkv * dh
    one = jnp.ones((dh,), F32)
    wq = w_in[:, :nq]
    qg = jnp.tile(q_norm * Q_PRESCALE, n_heads).reshape(1, D_MODEL)
    slopes = jnp.exp2(-8.0 * (jnp.arange(n_heads, dtype=F32) + 1.0) / n_heads) * LOG2_E
    s1 = slopes.astype(BF16).astype(F32)
    s2 = (slopes - s1).astype(BF16).astype(F32)
    s3 = (slopes - s1 - s2).astype(BF16).astype(F32)
    terms = jnp.stack([s1, s2, s3], axis=1)
    qc = jnp.concatenate([jnp.zeros((n_heads, dh), F32), terms, terms * ATT_TK,
                          jnp.zeros((n_heads, LANES - dh - 2 * N_POS_LANES), F32)], axis=1).reshape(1, QA_COLS)

    wkv = w_in[:, nq:nq + 6 * kv_cols].reshape(D_MODEL, 6, hkv, dh)
    wkva = jnp.stack([wkv[:, 2], wkv[:, 4], wkv[:, 3], wkv[:, 5]], axis=2).reshape(D_MODEL, 4 * kv_cols)
    wkvc = jnp.stack([wkv[:, 0], wkv[:, 1]], axis=2).reshape(D_MODEL, KVC_COLS)
    wg = _pad_cols(w_in[:, nq + 6 * kv_cols:], LANES)
    kg = jnp.tile(jnp.concatenate([k_norm[1], k_norm[2]]), hkv).reshape(1, hkv * LANES)
    q, kva, kvc, gt = _nsa_proj(h, seq, gn.reshape(1, D_MODEL), _bf(wq), _bf(wkva), _bf(wkvc), _bf(wg), qg, qc, kg)

    w1 = cmp_w1.reshape(2, 2 * CMP_STRIDE, dh, CMP_HID)
    zeros = jnp.zeros((2 * CMP_STRIDE, dh, CMP_HID), F32)
    wfull = jnp.concatenate([jnp.concatenate([w1[0], zeros], axis=2),
                             jnp.concatenate([zeros, w1[1]], axis=2)], axis=1)
    pe = jnp.concatenate([cmp_pe[0], cmp_pe[1]], axis=1)
    w2 = jnp.zeros((2 * CMP_HID, 2 * LANES), F32)
    w2 = w2.at[:CMP_HID, :dh].set(cmp_w2[0]).at[CMP_HID:, LANES:LANES + dh].set(cmp_w2[1])
    kg0 = jnp.concatenate([k_norm[0], one, one, one]).reshape(1, 2 * LANES)
    cmp = _nsa_cmp(batch, seq, kvc, _bf(wfull[:CMP_STRIDE]), _bf(wfull[CMP_STRIDE:]),
                   pe[:CMP_STRIDE], pe[CMP_STRIDE:], _bf(w2), kg0)

    key = jnp.arange(seq + ATT_TK)[:, None]
    lanes = jnp.arange(LANES)[None, :]
    onehot = jnp.where(key < seq, lanes == key // SLC_BLOCK, lanes == PAD_TILE_LANE)
    return _nsa_attn(batch, seq, q, kva, _bf(jnp.where(onehot, MASK_NEG, 0.0)), cmp, gt)


def kernel(x, mix_norm, mlp_norm, mlp_w1, mlp_w2, rwkv_mu, rwkv_w_in, rwkv_w0, rwkv_w1, rwkv_w2, rwkv_a0, rwkv_a1, rwkv_a2, rwkv_v0, rwkv_v1, rwkv_v2, rwkv_g1, rwkv_g2, rwkv_k_k, rwkv_k_a, rwkv_r_k, rwkv_ln_w, rwkv_ln_b, rwkv_w_out, nsa_w_in, nsa_cmp_pe, nsa_cmp_w1, nsa_cmp_w2, nsa_q_norm, nsa_k_norm, nsa_w_out):
    batch, seq, d = x.shape
    depth = mix_norm.shape[0]
    assert d == D_MODEL and RWKV_N == HEAD_DIM and NSA_DH == HEAD_DIM
    assert seq % ATT_TQ == 0 and seq >= (WINDOW // ATT_TK + 1) * ATT_TK and seq // SLC_BLOCK < PAD_TILE_LANE
    assert seq % WKV_TBLOCK == 0 and seq % RW_TM == 0 and (batch * seq) % MLP_TM == 0
    h = x.reshape(batch * seq, d)
    v_first = None
    for i in range(depth):
        j = i // 2
        if i % 2 == 0:
            vres = None if j == 0 else (rwkv_v0[j - 1], rwkv_v1[j - 1], rwkv_v2[j - 1])
            a, v_raw = _rwkv_layer(h, batch, seq, mix_norm[i], rwkv_mu[j], rwkv_w_in[j], rwkv_w0[j], rwkv_w1[j],
                                   rwkv_w2[j], rwkv_a0[j], rwkv_a1[j], rwkv_a2[j], rwkv_g1[j], rwkv_g2[j],
                                   rwkv_k_k[j], rwkv_k_a[j], rwkv_r_k[j].reshape(-1), rwkv_ln_w[j], rwkv_ln_b[j],
                                   v_first, vres)
            if j == 0:
                v_first = v_raw
            wo = rwkv_w_out[j]
        else:
            a = _nsa_layer(h, batch, seq, mix_norm[i], nsa_w_in[j], nsa_cmp_pe[j], nsa_cmp_w1[j], nsa_cmp_w2[j],
                           nsa_q_norm[j], nsa_k_norm[j])
            wo = nsa_w_out[j]
        h = _out_mlp(h, a, _bf(wo), mlp_norm[i].reshape(1, d), _bf(mlp_w1[i]), _bf(mlp_w2[i]))
    return h.reshape(batch, seq, d)
```

```python
import functools

import jax
import jax.numpy as jnp
import numpy as np
from jax import lax
from jax.experimental import pallas as pl
from jax.experimental.pallas import tpu as pltpu

F32 = jnp.float32
BF16 = jnp.bfloat16

D_MODEL = 1024
MLP_HIDDEN = 4 * D_MODEL
NORM_EPS = 1e-6
LANES = 128
SUBLANES = 8
HEAD_DIM = 64
HEAD_SHIFT = HEAD_DIM.bit_length() - 1
VMEM_LIMIT = 56 * 1024 * 1024

RWKV_N = 64
RWKV_GN_EPS = 64e-5
WKV_CHUNK = 64
WKV_TBLOCK = 256
WKV_PAIRS = 8

NSA_HKV = 4
NSA_G = 4
NSA_DH = 64
CMP_STRIDE = 16
CMP_HID = 128
SLC_BLOCK = 64
N_SELECT = 16
WINDOW = 512
SEL_BIG = 1e9
MASK_NEG = -1e30
LOG2_E = 1.4426950408889634
Q_PRESCALE = NSA_DH ** -0.5 * LOG2_E
PAD_TILE_LANE = LANES - 1


def _dot(a, b):
    return jnp.dot(a, b, preferred_element_type=F32)


def _dot_nt(a, b):
    return lax.dot_general(a, b, (((1,), (1,)), ((), ())), preferred_element_type=F32)


def _bf(x):
    return x.astype(BF16)


def _split2(x):
    hi = x.astype(BF16)
    lo = (x - hi.astype(F32)).astype(BF16)
    return hi, lo


def _dot_hl(x, w):
    hi, lo = _split2(x)
    return _dot(hi, w) + _dot(lo, w)


def _dot_hl_rev(w, x):
    hi, lo = _split2(x)
    return _dot(w, hi) + _dot(w, lo)


def _iota(shape, dim):
    return lax.broadcasted_iota(jnp.int32, shape, dim)


def _bd64(n=LANES):
    return jnp.where((_iota((n, n), 0) >> HEAD_SHIFT) == (_iota((n, n), 1) >> HEAD_SHIFT), 1.0, 0.0).astype(BF16)


def _seg64_sum(x, bd):
    outs = [_dot(_bf(x[:, c * LANES:(c + 1) * LANES]), bd) for c in range(x.shape[1] // LANES)]
    return outs[0] if len(outs) == 1 else jnp.concatenate(outs, axis=1)


def _rms(x, g):
    return x * lax.rsqrt(jnp.mean(x * x, axis=-1, keepdims=True) + NORM_EPS) * g


def _const_spec(shape):
    return pl.BlockSpec(shape, lambda *_: (0,) * len(shape))


def _params(sem):
    return pltpu.CompilerParams(dimension_semantics=sem, vmem_limit_bytes=VMEM_LIMIT)


MLP_TM = 512
MLP_TH = 1024


def _out_mlp_kernel(h_ref, a_ref, wo_ref, g_ref, w1_ref, w2_ref, o_ref):
    h1 = h_ref[...] + _dot(a_ref[...], wo_ref[...])
    xb = _bf(_rms(h1, g_ref[...]))
    acc = h1
    for c in range(MLP_HIDDEN // MLP_TH):
        u = jnp.maximum(_dot(xb, w1_ref[:, c * MLP_TH:(c + 1) * MLP_TH]), 0.0)
        acc = acc + _dot(_bf(u * u), w2_ref[c * MLP_TH:(c + 1) * MLP_TH, :])
    o_ref[...] = acc


def _out_mlp(h, a, wo, g, w1, w2):
    m = h.shape[0]
    row = lambda i: (i, 0)
    return pl.pallas_call(
        _out_mlp_kernel,
        grid=(m // MLP_TM,),
        in_specs=[pl.BlockSpec((MLP_TM, D_MODEL), row), pl.BlockSpec((MLP_TM, D_MODEL), row),
                  _const_spec((D_MODEL, D_MODEL)), _const_spec((1, D_MODEL)),
                  _const_spec((D_MODEL, MLP_HIDDEN)), _const_spec((MLP_HIDDEN, D_MODEL))],
        out_specs=pl.BlockSpec((MLP_TM, D_MODEL), row),
        out_shape=jax.ShapeDtypeStruct((m, D_MODEL), F32),
        compiler_params=_params(("parallel",)),
    )(h, a, wo, g, w1, w2)


RW_TM = 512
SHIFT_ROWS = 8


def _rwkv_prep_kernel(*refs, tiles_per_seq, vres):
    if vres:
        (h_ref, hp_ref, gn_ref, mu_ref, win_ref, w0_ref, w1_ref, w2_ref, a0_ref, a1_ref, a2_ref,
         g1_ref, g2_ref, kk_ref, ka_ref, vf_ref, v0_ref, v1_ref, v2_ref,
         r_o, lw_o, k_o, v_o, an_o, bb_o, g_o) = refs
    else:
        (h_ref, hp_ref, gn_ref, mu_ref, win_ref, w0_ref, w1_ref, w2_ref, a0_ref, a1_ref, a2_ref,
         g1_ref, g2_ref, kk_ref, ka_ref,
         r_o, lw_o, k_o, v_o, an_o, bb_o, g_o) = refs
    gn = gn_ref[...]
    xn = _rms(h_ref[...], gn)
    first = (pl.program_id(0) % tiles_per_seq) == 0
    prev = _rms(hp_ref[SHIFT_ROWS - 1:SHIFT_ROWS, :], gn)
    prev = jnp.where(first, 0.0, prev)
    xs = pltpu.roll(xn, 1, axis=0)
    xs = jnp.where(_iota(xn.shape, 0) == 0, prev, xs)
    dx = xs - xn

    def mixed(i):
        return _bf(xn + dx * mu_ref[i:i + 1, :])

    xa = mixed(4)
    a_low = _dot(xa, a1_ref[...])
    r_o[...] = _dot(mixed(0), win_ref[:, 0:D_MODEL])
    a = jax.nn.sigmoid(a0_ref[...] + _dot(_bf(a_low), a2_ref[...]))
    k = _dot(mixed(2), win_ref[:, D_MODEL:2 * D_MODEL])
    xw = mixed(1)
    w_low = jnp.tanh(_dot(xw, w1_ref[...]))
    kk = k * kk_ref[...]
    nrm = jnp.sqrt(_seg64_sum(kk * kk, _bd64()))
    k_o[...] = k * (1.0 + (a - 1.0) * ka_ref[...])
    z = w0_ref[...] + _dot(_bf(w_low), w2_ref[...])
    kk = kk / jnp.maximum(nrm, 1e-12)
    an_o[...] = -kk
    bb_o[...] = kk * a
    xv = mixed(3)
    v = _dot(xv, win_ref[:, 2 * D_MODEL:3 * D_MODEL])
    softplus = jnp.maximum(-z, 0.0) + jnp.log(1.0 + jnp.exp(-jnp.abs(z)))
    lw = -jnp.exp(-softplus - 0.5)
    if vres:
        lo = _dot(_bf(_dot(xv, v1_ref[...])), v2_ref[...])
        v = v + (vf_ref[...] - v) * jax.nn.sigmoid(v0_ref[...] + lo)
    v_o[...] = v
    g_low = jax.nn.sigmoid(_dot(mixed(5), g1_ref[...]))
    ii = _iota((RW_TM, RW_TM), 0)
    jj = _iota((RW_TM, RW_TM), 1)
    sh = WKV_CHUNK.bit_length() - 1
    cum_mat = jnp.where(((ii >> sh) == (jj >> sh)) & (jj <= ii), 1.0, 0.0).astype(BF16)
    lw_o[...] = _dot_hl_rev(cum_mat, lw)
    g_o[...] = _dot(_bf(g_low), g2_ref[...])


def _rwkv_prep(h, seq, gn, mu, win, w0, w1, w2, a0, a1, a2, g1, g2, k_k, k_a, vres):
    m = h.shape[0]
    row = lambda i: (i, 0)
    prev = lambda i: (jnp.maximum(i * (RW_TM // SHIFT_ROWS) - 1, 0), 0)
    vec = _const_spec((1, D_MODEL))
    full = lambda a: _const_spec(a.shape)
    args = [h, h, gn, mu, win, w0, w1, w2, a0, a1, a2, g1, g2, k_k, k_a]
    specs = [pl.BlockSpec((RW_TM, D_MODEL), row), pl.BlockSpec((SHIFT_ROWS, D_MODEL), prev), vec, full(mu),
             full(win), vec, full(w1), full(w2), vec, full(a1), full(a2), full(g1), full(g2), vec, vec]
    if vres is not None:
        v_first, v0, v1, v2 = vres
        args += [v_first, v0, v1, v2]
        specs += [pl.BlockSpec((RW_TM, D_MODEL), row), vec, full(v1), full(v2)]
    out = jax.ShapeDtypeStruct((m, D_MODEL), F32)
    return pl.pallas_call(
        functools.partial(_rwkv_prep_kernel, tiles_per_seq=seq // RW_TM, vres=vres is not None),
        grid=(m // RW_TM,),
        in_specs=specs,
        out_specs=[pl.BlockSpec((RW_TM, D_MODEL), row)] * 7,
        out_shape=[out] * 7,
        compiler_params=_params(("parallel",)),
    )(*args)


def _wkv_kernel(r_ref, cum_ref, k_ref, v_ref, a_ref, b_ref, g_ref, rk_ref, lnw_ref, lnb_ref,
                o_ref, s_ref, y_ref):
    L = WKV_CHUNK
    L2 = 2 * L

    @pl.when(pl.program_id(2) == 0)
    def _():
        s_ref[...] = jnp.zeros_like(s_ref)

    head0 = _iota((1, LANES), 1) < RWKV_N
    ii = _iota((L2, L2), 0)
    jj = _iota((L2, L2), 1)
    same = (ii >= L) == (jj >= L)
    tril_strict = same & (jj < ii)
    tril_incl = same & (jj <= ii)
    eye = jnp.where(ii == jj, 1.0, 0.0)
    first_row = _iota((L, LANES), 0) == 0
    bd = (_iota((LANES, LANES), 0) >> HEAD_SHIFT) == (_iota((LANES, LANES), 1) >> HEAD_SHIFT)

    def by_head(x):
        return jnp.concatenate([jnp.where(head0, x, 0.0), jnp.where(head0, 0.0, x)], axis=0)

    def twice(x):
        return jnp.concatenate([x, x], axis=0)

    def pick(x2):
        return jnp.where(head0, x2[:L], x2[L:])

    def chunk(sl, cols, s):
        cum = cum_ref[sl, cols]
        cum_ex = jnp.where(first_row, 0.0, pltpu.roll(cum, 1, axis=0))
        g_in = jnp.exp(cum)
        g_inv = jnp.exp(-cum)
        cum_l = cum[L - 1:L, :]
        g_to_end = jnp.exp(cum_l - cum)
        r = r_ref[sl, cols]
        k = k_ref[sl, cols]
        v = v_ref[sl, cols]
        b = b_ref[sl, cols]
        at = a_ref[sl, cols] * jnp.exp(cum_ex)
        rt = r * g_in
        bt = b * g_inv
        kt = k * g_inv
        sc = _dot_nt(_bf(jnp.concatenate([by_head(at), by_head(rt)], axis=0)),
                     _bf(jnp.concatenate([bt, bt, kt, kt], axis=0)))
        yield
        m_ab = jnp.where(tril_strict, sc[0:L2, 0:L2], 0.0)
        m_ak = jnp.where(tril_strict, sc[0:L2, L2:2 * L2], 0.0)
        n_rb = jnp.where(tril_incl, sc[L2:2 * L2, 0:L2], 0.0)
        n_rk = jnp.where(tril_incl, sc[L2:2 * L2, L2:2 * L2], 0.0)
        inv = eye + m_ab
        pw = _dot(_bf(m_ab), _bf(m_ab))
        yield
        for _ in range(int(np.log2(L)) - 2):
            both = _dot(_bf(jnp.concatenate([pw, inv], axis=0)), _bf(pw))
            yield
            pw, inv = both[0:L2], inv + both[L2:2 * L2]
        inv = inv + _dot(_bf(inv), _bf(pw))
        yield
        ps = _dot_nt(_bf(jnp.concatenate([at, rt], axis=0)), _bf(s))
        yield
        mv = _dot(_bf(jnp.concatenate([m_ak, n_rk], axis=0)), _bf(twice(v)))
        yield
        u = pick(_dot(_bf(inv), _bf(twice(ps[:L]) + mv[0:L2])))
        yield
        y_ref[sl, cols] = pick(twice(ps[L:]) + _dot(_bf(n_rb), _bf(twice(u))) + mv[L2:2 * L2])
        yield
        uv = jnp.concatenate([u, v], axis=0)
        bk = jnp.concatenate([b * g_to_end, k * g_to_end], axis=0)
        return s * jnp.exp(cum_l) + jnp.where(bd, _dot(_bf(uv.T), _bf(bk)), 0.0)

    ones = jnp.where(bd, 1.0, 0.0).astype(BF16)

    def pair(p):
        cols = slice(p * LANES, (p + 1) * LANES)
        s = s_ref[p]
        for c in range(WKV_TBLOCK // L):
            s = yield from chunk(pl.ds(c * L, L), cols, s)
            yield
        s_ref[p] = s
        y = y_ref[:, cols]
        mean = _dot_hl(y, ones) * (1.0 / RWKV_N)
        yield
        d = y - mean
        var = _dot(_bf(d * d), ones) * (1.0 / RWKV_N)
        yield
        yn = d * lax.rsqrt(var + RWKV_GN_EPS) * lnw_ref[:, cols] + lnb_ref[:, cols]
        bonus = _dot(_bf(r_ref[:, cols] * k_ref[:, cols] * rk_ref[:, cols]), ones) * v_ref[:, cols]
        o_ref[:, cols] = _bf((yn + bonus) * g_ref[:, cols])

    running = [pair(p) for p in range(WKV_PAIRS)]
    done = object()
    while running:
        running = [gen for gen in running if next(gen, done) is not done]


def _wkv(batch, seq, r, cum, k, v, an, bb, g, r_k, ln_w, ln_b):
    m = r.shape[0]
    nt = seq // WKV_TBLOCK
    width = WKV_PAIRS * LANES
    blk = pl.BlockSpec((WKV_TBLOCK, width), lambda b, p, i: (b * nt + i, p))
    vec = pl.BlockSpec((1, width), lambda b, p, i: (0, p))
    return pl.pallas_call(
        _wkv_kernel,
        grid=(batch, D_MODEL // width, nt),
        in_specs=[blk] * 7 + [vec] * 3,
        out_specs=blk,
        out_shape=jax.ShapeDtypeStruct((m, D_MODEL), BF16),
        scratch_shapes=[pltpu.VMEM((WKV_PAIRS, LANES, LANES), F32), pltpu.VMEM((WKV_TBLOCK, width), F32)],
        compiler_params=_params(("parallel", "parallel", "arbitrary")),
    )(r, cum, k, v, an, bb, g, r_k, ln_w, ln_b)


ATT_TQ = 256
ATT_TK = 256
NSA_TM = 2 * ATT_TK
N_POS_LANES = 3
KVA_HEAD_COLS = 4 * LANES
KVA_COLS = NSA_HKV * KVA_HEAD_COLS
KVC_COLS = NSA_HKV * LANES
QA_COLS = NSA_HKV * NSA_G * LANES
GATE_COLS = NSA_HKV * LANES


def _nsa_proj_kernel(h_ref, gn_ref, wq_ref, wkv_ref, wkvc_ref, wg_ref, qg_ref, qc_ref, kg_ref,
                     q_o, kva_o, kvc_o, gt_o, *, tiles_per_seq):
    xb = _bf(_rms(h_ref[...], gn_ref[...]))
    bd = _bd64()
    inv_dh = 1.0 / NSA_DH
    lane = _iota((NSA_TM, LANES), 1)
    low = lane < NSA_DH

    def spread(pair, fill_a, fill_b):
        return [jnp.where(low, pair, fill_a), jnp.where(low, pltpu.roll(pair, NSA_DH, axis=1), fill_b)]

    q = _dot(xb, wq_ref[...])
    q = q * lax.rsqrt(_seg64_sum(q * q, bd) * inv_dh + NORM_EPS) * qg_ref[...]
    outs = []
    for c in range(D_MODEL // LANES):
        outs += spread(q[:, c * LANES:(c + 1) * LANES], qc_ref[:, 2 * c * LANES:(2 * c + 1) * LANES],
                       qc_ref[:, (2 * c + 1) * LANES:(2 * c + 2) * LANES])
    q_o[...] = _bf(jnp.concatenate(outs, axis=1))

    row = _iota((NSA_TM, LANES), 0)
    tk_shift = ATT_TK.bit_length() - 1
    pos_mod = (row & (ATT_TK - 1)).astype(F32)
    pos_tile = ((pl.program_id(0) % tiles_per_seq) * (NSA_TM // ATT_TK) + (row >> tk_shift)).astype(F32)
    key_fill = jnp.where(lane < NSA_DH + N_POS_LANES, pos_mod, jnp.where(lane < NSA_DH + 2 * N_POS_LANES, pos_tile, 0.0))
    val_fill = jnp.where(lane == NSA_DH, 1.0, 0.0)
    kv = _dot(xb, wkv_ref[...])
    outs = []
    for hd in range(NSA_HKV):
        keys = kv[:, 2 * hd * LANES:(2 * hd + 1) * LANES]
        keys = keys * lax.rsqrt(_dot(_bf(keys * keys), bd) * inv_dh + NORM_EPS) * kg_ref[:, hd * LANES:(hd + 1) * LANES]
        outs += spread(keys, key_fill, key_fill)
        outs += spread(kv[:, (2 * hd + 1) * LANES:(2 * hd + 2) * LANES], val_fill, val_fill)
    kva_o[...] = _bf(jnp.concatenate(outs, axis=1))

    kvc_o[...] = _dot(xb, wkvc_ref[...])
    gates = jax.nn.sigmoid(_dot(xb, wg_ref[...]))
    per_head = 3 * NSA_G
    gt_o[...] = jnp.concatenate([gates if hd == 0 else pltpu.roll(gates, LANES - per_head * hd, axis=1)
                                 for hd in range(NSA_HKV)], axis=1)


def _nsa_proj(h, seq, gn, wq, wkv, wkvc, wg, qg, qc, kg):
    m = h.shape[0]
    row = lambda i: (i, 0)
    return pl.pallas_call(
        functools.partial(_nsa_proj_kernel, tiles_per_seq=seq // NSA_TM),
        grid=(m // NSA_TM,),
        in_specs=[pl.BlockSpec((NSA_TM, D_MODEL), row), _const_spec((1, D_MODEL)),
                  _const_spec(wq.shape), _const_spec(wkv.shape), _const_spec(wkvc.shape), _const_spec(wg.shape),
                  _const_spec(qg.shape), _const_spec(qc.shape), _const_spec(kg.shape)],
        out_specs=[pl.BlockSpec((NSA_TM, QA_COLS), row), pl.BlockSpec((NSA_TM, KVA_COLS), row),
                   pl.BlockSpec((NSA_TM, KVC_COLS), row), pl.BlockSpec((NSA_TM, GATE_COLS), row)],
        out_shape=[jax.ShapeDtypeStruct((m, QA_COLS), BF16), jax.ShapeDtypeStruct((m, KVA_COLS), BF16),
                   jax.ShapeDtypeStruct((m, KVC_COLS), F32), jax.ShapeDtypeStruct((m, GATE_COLS), F32)],
        compiler_params=_params(("parallel",)),
    )(h, gn, wq, wkv, wkvc, wg, qg, qc, kg)


def _nsa_cmp_kernel(kv_ref, wa_ref, wb_ref, pa_ref, pb_ref, w2_ref, kg_ref, o_ref, *, n_blk):
    nb = n_blk + 1
    first = jnp.zeros((nb, 2 * CMP_HID), F32)
    second = jnp.zeros((nb, 2 * CMP_HID), F32)
    for p in range(CMP_STRIDE):
        x = kv_ref[pl.ds(p, nb, stride=CMP_STRIDE), :]
        first = first + _dot(_bf(x + pa_ref[p:p + 1, :]), wa_ref[p])
        second = second + _dot(_bf(x + pb_ref[p:p + 1, :]), wb_ref[p])
    hid = jax.nn.gelu(first + pltpu.roll(second, nb - 1, axis=0))
    out = _dot(_bf(hid), w2_ref[...])
    ss = _seg64_sum(out * out, _bd64()) * (1.0 / NSA_DH)
    lane = _iota((nb, 2 * LANES), 1)
    out = jnp.where(lane < NSA_DH, out * lax.rsqrt(ss + NORM_EPS) * kg_ref[...], out)
    end = _iota((nb, 2 * LANES), 0) * CMP_STRIDE + (2 * CMP_STRIDE - 1)
    sh = ATT_TK.bit_length() - 1
    out = jnp.where((lane >= NSA_DH) & (lane < NSA_DH + N_POS_LANES), (end & (ATT_TK - 1)).astype(F32), out)
    out = jnp.where((lane >= NSA_DH + N_POS_LANES) & (lane < NSA_DH + 2 * N_POS_LANES), (end >> sh).astype(F32), out)
    o_ref[...] = _bf(out)


def _nsa_cmp(batch, seq, kv, wa, wb, pa, pb, w2, kg):
    n_blk = seq // CMP_STRIDE - 1
    nb = n_blk + 1
    return pl.pallas_call(
        functools.partial(_nsa_cmp_kernel, n_blk=n_blk),
        grid=(batch, NSA_HKV),
        in_specs=[pl.BlockSpec((seq, LANES), lambda b, h: (b, h)),
                  _const_spec(wa.shape), _const_spec(wb.shape), _const_spec(pa.shape), _const_spec(pb.shape),
                  _const_spec(w2.shape), _const_spec(kg.shape)],
        out_specs=pl.BlockSpec((None, None, nb, 2 * LANES), lambda b, h: (b, h, 0, 0)),
        out_shape=jax.ShapeDtypeStruct((batch, NSA_HKV, nb, 2 * LANES), BF16),
        compiler_params=_params(("parallel", "parallel")),
    )(kv, wa, wb, pa, pb, w2, kg)


def _nsa_attn_kernel(q_ref, kv_ref, oh_ref, cmp_ref, gt_ref, o_ref,
                     qam_s, m_s, acc_s, sa_s, sb_s, ow_s, tiles_s, *, n_blk, n_sel, n_slc):
    tq, tk = ATT_TQ, ATT_TK
    R = NSA_G * tq
    qi = pl.program_id(2)
    q0 = qi * tq
    t_one = q0 + _iota((tq, 1), 0)
    gt = gt_ref[...]
    n_ch = NSA_G
    qa = jnp.concatenate([q_ref[:, g * LANES:(g + 1) * LANES] for g in range(NSA_G)], axis=0)
    qam_s[:, 0:LANES] = qa

    n_win = WINDOW // tk + 1
    wrows = pl.ds(pl.multiple_of(jnp.maximum(qi - (n_win - 1), 0) * tk, tk), n_win * tk)

    def window():
        dist = t_one - (wrows.start + _iota((1, n_win * tk), 1))
        in_window = dist.astype(jnp.uint32) < WINDOW
        s_all = _dot_nt(qa, kv_ref[wrows, LANES:2 * LANES])
        yield
        pes = []
        for c in range(n_ch):
            s = jnp.where(in_window, s_all[c * tq:(c + 1) * tq], MASK_NEG)
            pes.append(_bf(jnp.exp2(s - jnp.max(s, axis=-1, keepdims=True))))
            yield
        pv = _dot(jnp.concatenate(pes, axis=0), kv_ref[wrows, 3 * LANES:4 * LANES]).reshape(NSA_G, tq, LANES)
        yield
        for g in range(NSA_G):
            ow_s[g] = ((gt[:, 3 * g + 2:3 * g + 3] / pv[g][:, NSA_DH:NSA_DH + 1]) * pv[g][:, 0:NSA_DH]
                       + gt[:, 3 * g:3 * g + 1] * picked["o_cmp"][g * tq:(g + 1) * tq])

    picked = {}

    def compressed():
        nb = cmp_ref.shape[0]
        n_idx = _iota((1, nb), 1)
        ps = []
        valid = (n_idx * CMP_STRIDE + (2 * CMP_STRIDE - 1) <= t_one) & (n_idx < n_blk)
        qk = _dot_nt(qa, cmp_ref[:, 0:LANES])
        yield
        for g in range(NSA_G):
            s = jnp.where(valid, qk[g * tq:(g + 1) * tq], -jnp.inf)
            mx = jnp.max(s, axis=-1, keepdims=True)
            mx = jnp.where(mx == -jnp.inf, 0.0, mx)
            e = jnp.exp2(s - mx)
            ps.append(e * (1.0 / jnp.maximum(jnp.sum(e, axis=-1, keepdims=True), 1e-30)))
            yield
        picked["o_cmp"] = _dot(_bf(jnp.concatenate(ps, axis=0)), cmp_ref[:, LANES:LANES + NSA_DH])
        yield
        psum = ps[0] + ps[1] + ps[2] + ps[3]
        sj = _iota((n_slc, nb), 0) * SLC_BLOCK
        cn = _iota((n_slc, nb), 1) * CMP_STRIDE
        overlap_t = jnp.where((cn <= sj + SLC_BLOCK - 1) & (cn + 2 * CMP_STRIDE - 1 >= sj), 1.0, 0.0).astype(BF16)
        p_hi, p_lo = _split2(psum)
        imp = _dot_nt(overlap_t, p_hi) + _dot_nt(overlap_t, p_lo)
        yield
        blk = _iota((n_slc, 1), 0)
        ahead_of_cur = ((q0 + _iota((1, tq), 1)) >> (SLC_BLOCK.bit_length() - 1)) - blk
        forced = (blk * ahead_of_cur * (ahead_of_cur - 1)) == 0
        imp = jnp.where(forced, SEL_BIG, imp)
        imp = jnp.where(ahead_of_cur >= 0, imp, -SEL_BIG)
        sub = SUBLANES
        ranks = []
        for lo in range(0, n_slc, sub):
            mine = imp[lo:lo + sub, :]
            rk = jnp.zeros((sub, tq), F32)
            for j in range(n_slc):
                other = imp[j:j + 1, :]
                if j < lo:
                    rk = rk + jnp.where(other >= mine, 1.0, 0.0)
                elif j >= lo + sub:
                    rk = rk + jnp.where(other > mine, 1.0, 0.0)
                else:
                    rk = rk + jnp.where(blk[lo:lo + sub] > j, jnp.where(other >= mine, 1.0, 0.0),
                                        jnp.where(other > mine, 1.0, 0.0))
            ranks.append(rk)
            yield
        rank = jnp.concatenate(ranks, axis=0)
        picked["not_sel_t"] = jnp.where(rank < n_sel, 0.0, 1.0)

    branches = {"w": window(), "c": compressed()}
    for name in "c" + "w" + "cccc" + "cc" + "wwww" + "w" + "c" * (n_slc // 8) + "cw":
        next(branches[name], None)
    assert all(next(branch, "finished") == "finished" for branch in branches.values())
    not_sel_t = picked["not_sel_t"]
    qm = jnp.concatenate([not_sel_t.T, jnp.zeros((tq, LANES - n_slc), F32)], axis=1)
    qm = _bf(jnp.where(_iota((tq, LANES), 1) == PAD_TILE_LANE, 1.0, qm))
    for g in range(NSA_G):
        qam_s[g * tq:(g + 1) * tq, LANES:2 * LANES] = qm

    n_kt = n_slc * SLC_BLOCK // tk
    in_tile = (_iota((n_kt, n_slc), 1) * SLC_BLOCK // tk) == _iota((n_kt, n_slc), 0)
    picks = _dot(jnp.where(in_tile, 1.0, 0.0).astype(BF16), _bf(1.0 - not_sel_t))
    needed = jnp.max(picks, axis=1, keepdims=True)
    n_int = jnp.int32(0)
    for kt in range(n_kt):
        tiles_s[n_int] = kt
        n_int = n_int + ((needed[kt, 0] > 0.0) & (kt < qi)).astype(jnp.int32)
    n_even = n_int + (n_int & 1)
    tiles_s[n_int] = n_kt
    tiles_s[n_even] = qi
    tiles_s[n_even + 1] = qi

    m_s[...] = jnp.full_like(m_s, MASK_NEG)
    acc_s[...] = jnp.zeros_like(acc_s)

    def key_rows(j):
        kt = tiles_s[j]
        return (pl.ds(pl.multiple_of(jnp.minimum(kt, n_kt - 1) * tk, tk), tk),
                pl.ds(pl.multiple_of(kt * tk, tk), tk))

    def scores(j, s_ref):
        kv_rows, oh_rows = key_rows(j)
        kaug = jnp.concatenate([kv_ref[kv_rows, 0:LANES], oh_ref[oh_rows, :]], axis=1)
        s_ref[...] = _dot_nt(qam_s[...], kaug)

    def softmax_pv(j, s_ref, diagonal):
        kv_rows, _ = key_rows(j)
        pes, alphas = [], []
        causal = q0 + _iota((1, tk), 1) <= t_one
        for c in range(n_ch):
            s = s_ref[c * tq:(c + 1) * tq, :]
            if diagonal:
                s = jnp.where(causal, s, MASK_NEG)
            m_old = m_s[c]
            m_new = jnp.maximum(m_old, jnp.max(s, axis=-1, keepdims=True))
            pes.append(_bf(jnp.exp2(s - jnp.concatenate([m_new] * (tk // LANES), axis=1))))
            alphas.append(jnp.exp2(m_old - m_new))
            m_s[c] = m_new
        pv = _dot(jnp.concatenate(pes, axis=0), kv_ref[kv_rows, 2 * LANES:3 * LANES])
        for c in range(n_ch):
            acc_s[c] = alphas[c] * acc_s[c] + pv[c * tq:(c + 1) * tq]

    def sel_body(jj, carry):
        j = 2 * jj
        scores(j + 1, sb_s)
        softmax_pv(j, sa_s, False)
        scores(j + 2, sa_s)
        softmax_pv(j + 1, sb_s, False)
        return carry

    scores(0, sa_s)
    lax.fori_loop(0, n_even // 2, sel_body, 0)
    softmax_pv(n_even, sa_s, True)
    acc = acc_s[...].reshape(NSA_G, tq, LANES)

    outs = []
    for g in range(NSA_G):
        w_sel = gt[:, 3 * g + 1:3 * g + 2] / acc[g][:, NSA_DH:NSA_DH + 1]
        outs.append(w_sel * acc[g][:, 0:NSA_DH] + ow_s[g])
    o_ref[...] = _bf(jnp.concatenate(outs, axis=1))


def _nsa_attn(batch, seq, q, kva, onehot, cmp, gt):
    assert ATT_TQ == ATT_TK
    m = q.shape[0]
    nq = seq // ATT_TQ
    n_blk = seq // CMP_STRIDE - 1
    n_slc = seq // SLC_BLOCK
    n_sel = min(N_SELECT, n_slc)
    R = NSA_G * ATT_TQ
    tile = lambda b, h, i: (b * nq + i, h)
    return pl.pallas_call(
        functools.partial(_nsa_attn_kernel, n_blk=n_blk, n_sel=n_sel, n_slc=n_slc),
        grid=(batch, NSA_HKV, nq),
        in_specs=[pl.BlockSpec((ATT_TQ, NSA_G * LANES), tile),
                  pl.BlockSpec((seq, KVA_HEAD_COLS), lambda b, h, i: (b, h)),
                  pl.BlockSpec((seq + ATT_TK, LANES), lambda b, h, i: (0, 0)),
                  pl.BlockSpec((None, None, n_blk + 1, 2 * LANES), lambda b, h, i: (b, h, 0, 0)),
                  pl.BlockSpec((ATT_TQ, LANES), tile)],
        out_specs=pl.BlockSpec((ATT_TQ, NSA_G * NSA_DH), tile),
        out_shape=jax.ShapeDtypeStruct((m, D_MODEL), BF16),
        scratch_shapes=[pltpu.VMEM((R, 2 * LANES), BF16),
                        pltpu.VMEM((NSA_G, ATT_TQ, LANES), F32),
                        pltpu.VMEM((NSA_G, ATT_TQ, LANES), F32),
                        pltpu.VMEM((R, ATT_TK), F32), pltpu.VMEM((R, ATT_TK), F32),
                        pltpu.VMEM((NSA_G, ATT_TQ, NSA_DH), F32),
                        pltpu.SMEM((seq // ATT_TK + 3,), jnp.int32)],
        compiler_params=_params(("parallel", "parallel", "arbitrary")),
    )(q, kva, onehot, cmp, gt)


def _pad_cols(w, n):
    return jnp.pad(w, ((0, 0), (0, n - w.shape[1])))


def _pad_rows(w, n):
    return jnp.pad(w, ((0, n - w.shape[0]), (0, 0)))


def _lora(w_down, w_up):
    rank = -(-w_down.shape[1] // LANES) * LANES
    return _bf(_pad_cols(w_down, rank)), _bf(_pad_rows(w_up, rank))


def _rwkv_layer(h, batch, seq, gn, mu, w_in, w0, w1, w2, a0, a1, a2, g1, g2, k_k, k_a, r_k, ln_w, ln_b,
                v_first, vres):
    vec = lambda x: x.reshape(1, D_MODEL)
    w1p, w2p = _lora(w1, w2)
    a1p, a2p = _lora(a1, a2)
    g1p, g2p = _lora(g1, g2)
    if vres is not None:
        v0, v1, v2 = vres
        v1p, v2p = _lora(v1, v2)
        vres = (v_first, vec(v0), v1p, v2p)
    mu8 = jnp.pad(mu, ((0, 2), (0, 0)))
    r, lw, k, v, an, bb, g = _rwkv_prep(h, seq, vec(gn), mu8, _bf(w_in), vec(w0), w1p, w2p, vec(a0), a1p, a2p,
                                        g1p, g2p, vec(k_k), vec(k_a), vres)
    y = _wkv(batch, seq, r, lw, k, v, an, bb, g, vec(r_k), vec(ln_w), vec(ln_b))
    return y, v


def _nsa_layer(h, batch, seq, gn, w_in, cmp_pe, cmp_w1, cmp_w2, q_norm, k_norm):
    hkv, dh = NSA_HKV, NSA_DH
    nq = D_MODEL
    n_heads = hkv * NSA_G
    kv_cols = hkv * dh
    one = jnp.ones((dh,), F32)
    wq = w_in[:, :nq]
    qg = jnp.tile(q_norm * Q_PRESCALE, n_heads).reshape(1, D_MODEL)
    slopes = jnp.exp2(-8.0 * (jnp.arange(n_heads, dtype=F32) + 1.0) / n_heads) * LOG2_E
    s1 = slopes.astype(BF16).astype(F32)
    s2 = (slopes - s1).astype(BF16).astype(F32)
    s3 = (slopes - s1 - s2).astype(BF16).astype(F32)
    terms = jnp.stack([s1, s2, s3], axis=1)
    qc = jnp.concatenate([jnp.zeros((n_heads, dh), F32), terms, terms * ATT_TK,
                          jnp.zeros((n_heads, LANES - dh - 2 * N_POS_LANES), F32)], axis=1).reshape(1, QA_COLS)

    wkv = w_in[:, nq:nq + 6 * kv_cols].reshape(D_MODEL, 6, hkv, dh)
    wkva = jnp.stack([wkv[:, 2], wkv[:, 4], wkv[:, 3], wkv[:, 5]], axis=2).reshape(D_MODEL, 4 * kv_cols)
    wkvc = jnp.stack([wkv[:, 0], wkv[:, 1]], axis=2).reshape(D_MODEL, KVC_COLS)
    wg = _pad_cols(w_in[:, nq + 6 * kv_cols:], LANES)
    kg = jnp.tile(jnp.concatenate([k_norm[1], k_norm[2]]), hkv).reshape(1, hkv * LANES)
    q, kva, kvc, gt = _nsa_proj(h, seq, gn.reshape(1, D_MODEL), _bf(wq), _bf(wkva), _bf(wkvc), _bf(wg), qg, qc, kg)

    w1 = cmp_w1.reshape(2, 2 * CMP_STRIDE, dh, CMP_HID)
    zeros = jnp.zeros((2 * CMP_STRIDE, dh, CMP_HID), F32)
    wfull = jnp.concatenate([jnp.concatenate([w1[0], zeros], axis=2),
                             jnp.concatenate([zeros, w1[1]], axis=2)], axis=1)
    pe = jnp.concatenate([cmp_pe[0], cmp_pe[1]], axis=1)
    w2 = jnp.zeros((2 * CMP_HID, 2 * LANES), F32)
    w2 = w2.at[:CMP_HID, :dh].set(cmp_w2[0]).at[CMP_HID:, LANES:LANES + dh].set(cmp_w2[1])
    kg0 = jnp.concatenate([k_norm[0], one, one, one]).reshape(1, 2 * LANES)
    cmp = _nsa_cmp(batch, seq, kvc, _bf(wfull[:CMP_STRIDE]), _bf(wfull[CMP_STRIDE:]),
                   pe[:CMP_STRIDE], pe[CMP_STRIDE:], _bf(w2), kg0)

    key = jnp.arange(seq + ATT_TK)[:, None]
    lanes = jnp.arange(LANES)[None, :]
    onehot = jnp.where(key < seq, lanes == key // SLC_BLOCK, lanes == PAD_TILE_LANE)
    return _nsa_attn(batch, seq, q, kva, _bf(jnp.where(onehot, MASK_NEG, 0.0)), cmp, gt)


def kernel(x, mix_norm, mlp_norm, mlp_w1, mlp_w2, rwkv_mu, rwkv_w_in, rwkv_w0, rwkv_w1, rwkv_w2, rwkv_a0, rwkv_a1, rwkv_a2, rwkv_v0, rwkv_v1, rwkv_v2, rwkv_g1, rwkv_g2, rwkv_k_k, rwkv_k_a, rwkv_r_k, rwkv_ln_w, rwkv_ln_b, rwkv_w_out, nsa_w_in, nsa_cmp_pe, nsa_cmp_w1, nsa_cmp_w2, nsa_q_norm, nsa_k_norm, nsa_w_out):
    batch, seq, d = x.shape
    depth = mix_norm.shape[0]
    assert d == D_MODEL and RWKV_N == HEAD_DIM and NSA_DH == HEAD_DIM
    assert seq % ATT_TQ == 0 and seq >= (WINDOW // ATT_TK + 1) * ATT_TK and seq // SLC_BLOCK < PAD_TILE_LANE
    assert seq % WKV_TBLOCK == 0 and seq % RW_TM == 0 and seq % NSA_TM == 0 and (batch * seq) % MLP_TM == 0
    h = x.reshape(batch * seq, d)
    v_first = None
    for i in range(depth):
        j = i // 2
        if i % 2 == 0:
            vres = None if j == 0 else (rwkv_v0[j - 1], rwkv_v1[j - 1], rwkv_v2[j - 1])
            a, v_raw = _rwkv_layer(h, batch, seq, mix_norm[i], rwkv_mu[j], rwkv_w_in[j], rwkv_w0[j], rwkv_w1[j],
                                   rwkv_w2[j], rwkv_a0[j], rwkv_a1[j], rwkv_a2[j], rwkv_g1[j], rwkv_g2[j],
                                   rwkv_k_k[j], rwkv_k_a[j], rwkv_r_k[j].reshape(-1), rwkv_ln_w[j], rwkv_ln_b[j],
                                   v_first, vres)
            if j == 0:
                v_first = v_raw
            wo = rwkv_w_out[j]
        else:
            a = _nsa_layer(h, batch, seq, mix_norm[i], nsa_w_in[j], nsa_cmp_pe[j], nsa_cmp_w1[j], nsa_cmp_w2[j],
                           nsa_q_norm[j], nsa_k_norm[j])
            wo = nsa_w_out[j]
        h = _out_mlp(h, a, _bf(wo), mlp_norm[i].reshape(1, d), _bf(mlp_w1[i]), _bf(mlp_w2[i]))
    return h.reshape(batch, seq, d)
```
